```python
import functools
import math
import jax, jax.numpy as jnp
from jax import lax
import numpy as np

D_MODEL = 2048
BATCH = 2
SEQ = 4096
DEPTH = 2
DEC_BATCH = 128
DEC_SEQ = 8
PAST_LEN = 16384
PAGE_SIZE = 128

MLA_HEADS = 8
NOPE_DIM = 128
ROPE_DIM = 64
V_DIM = 128
Q_RANK = 512
KV_RANK = 256
ROPE_THETA = 10000.0
Q_BLOCK = 128
ATTN_SCALE = (NOPE_DIM + ROPE_DIM) ** -0.5
SSM_GROUPS = 32
SSM_GROUP_DIM = 16
SSM_WIDTH = SSM_GROUPS * SSM_GROUP_DIM
SSM_STATE = 64
DT_MIN = 0.001
DT_MAX = 0.1
GMLP_HEADS = 4
GMLP_HEAD_DIM = 128
GMLP_WIDTH = GMLP_HEADS * GMLP_HEAD_DIM
CHUNK = 128
ATTN_WIDTH = MLA_HEADS * V_DIM
MIX_WIDTH = ATTN_WIDTH + SSM_WIDTH + GMLP_WIDTH
IN_WIDTH = Q_RANK + KV_RANK + ROPE_DIM + SSM_WIDTH + 2 * GMLP_WIDTH
IN_SPLITS = (Q_RANK, Q_RANK + KV_RANK, Q_RANK + KV_RANK + ROPE_DIM,
             Q_RANK + KV_RANK + ROPE_DIM + SSM_WIDTH)
N_EXPERTS = 64
TOP_K = 8
N_EXPERT_GROUPS = 8
TOPK_GROUPS = 4
EXPERT_FF = 512
SHARED_FF = 512
ROUTE_SCALE = 2.5
MOE_BLOCK = 128
DEEPNORM_ALPHA = (2.0 * DEPTH) ** 0.25
DEEPNORM_BETA = (8.0 * DEPTH) ** -0.25
EPS = 1e-6

kernel_name = 'hymba_mla_s5_gmlp_moe_step'


def _layer_norm(x, g, b):
    xf = x.astype(jnp.float32)
    mu = jnp.mean(xf, axis=-1, keepdims=True)
    var = jnp.mean(jnp.square(xf - mu), axis=-1, keepdims=True)
    y = (xf - mu) * lax.rsqrt(var + EPS) * g.astype(jnp.float32) + b.astype(jnp.float32)
    return y.astype(x.dtype)


def _rms_norm(x, g):
    xf = x.astype(jnp.float32)
    y = xf * lax.rsqrt(jnp.mean(jnp.square(xf), axis=-1, keepdims=True) + EPS) * g.astype(jnp.float32)
    return y.astype(x.dtype)


def _rope(x, pos):
    half = ROPE_DIM // 2
    inv_freq = ROPE_THETA ** (-jnp.arange(half, dtype=jnp.float32) / half)
    ang = pos.astype(jnp.float32)[:, None] * inv_freq[None, :]
    cos = jnp.cos(ang)[:, None, :]
    sin = jnp.sin(ang)[:, None, :]
    xf = x.astype(jnp.float32)
    x1, x2 = xf[..., :half], xf[..., half:]
    return jnp.concatenate([x1 * cos - x2 * sin, x2 * cos + x1 * sin], axis=-1).astype(x.dtype)


def _mla_prepare(c_q, c_kv, k_r, pos, q_norm_g, kv_norm_g, w_uq, w_uk):
    b, s, _ = c_q.shape
    q = (_rms_norm(c_q, q_norm_g) @ w_uq).reshape(b, s, MLA_HEADS, NOPE_DIM + ROPE_DIM)
    q_nope = q[..., :NOPE_DIM]
    q_rope = _rope(q[..., NOPE_DIM:], pos)
    q_lat = jnp.einsum('bshn,rhn->bshr', q_nope, w_uk)
    c = _rms_norm(c_kv, kv_norm_g)
    kr = _rope(k_r[:, :, None, :], pos)[:, :, 0, :]
    return q_lat, q_rope, c, kr


def _mla_attend_self(q_lat, q_rope, c, kr):
    b, s = q_lat.shape[:2]
    nb = s // Q_BLOCK
    ql = jnp.moveaxis(q_lat.reshape(b, nb, Q_BLOCK, MLA_HEADS, KV_RANK), 1, 0)
    qr = jnp.moveaxis(q_rope.reshape(b, nb, Q_BLOCK, MLA_HEADS, ROPE_DIM), 1, 0)
    qpos = jnp.arange(s, dtype=jnp.int32).reshape(nb, Q_BLOCK)
    kpos = jnp.arange(s, dtype=jnp.int32)

    def block(args):
        qlb, qrb, qp = args
        sc = jnp.einsum('bqhr,bkr->bhqk', qlb, c) + jnp.einsum('bqhd,bkd->bhqk', qrb, kr)
        sc = jnp.where(kpos[None, :] <= qp[:, None], sc.astype(jnp.float32) * ATTN_SCALE, -jnp.inf)
        p = jax.nn.softmax(sc, axis=-1).astype(c.dtype)
        return jnp.einsum('bhqk,bkr->bqhr', p, c)

    o = lax.map(block, (ql, qr, qpos))
    return jnp.moveaxis(o, 0, 1).reshape(b, s, MLA_HEADS, KV_RANK)


def _mla_attend_cached(q_lat, q_rope, c_new, kr_new, c_past, kr_past):
    s = q_lat.shape[1]
    past = c_past.shape[1]
    s_past = jnp.einsum('bqhr,bkr->bhqk', q_lat, c_past) + jnp.einsum('bqhd,bkd->bhqk', q_rope, kr_past)
    s_new = jnp.einsum('bqhr,bkr->bhqk', q_lat, c_new) + jnp.einsum('bqhd,bkd->bhqk', q_rope, kr_new)
    causal = jnp.tril(jnp.ones((s, s), dtype=bool))
    s_new = jnp.where(causal, s_new.astype(jnp.float32), -jnp.inf)
    sc = jnp.concatenate([s_past.astype(jnp.float32), s_new], axis=-1) * ATTN_SCALE
    p = jax.nn.softmax(sc, axis=-1).astype(c_new.dtype)
    return (jnp.einsum('bhqk,bkr->bqhr', p[..., :past], c_past)
            + jnp.einsum('bhqk,bkr->bqhr', p[..., past:], c_new))


def _complex_affine_combine(e1, e2):
    a1r, a1i, b1r, b1i = e1
    a2r, a2i, b2r, b2i = e2
    return (a2r * a1r - a2i * a1i, a2r * a1i + a2i * a1r,
            a2r * b1r - a2i * b1i + b2r, a2r * b1i + a2i * b1r + b2i)


def _s5(u, h0_re, h0_im, log_dt, a_re, a_im, b_re, b_im, c_re, c_im, d, w_glu, b_glu):
    f32 = jnp.float32
    bsz, s, _ = u.shape
    a_re, a_im = a_re.astype(f32), a_im.astype(f32)
    b_re, b_im = b_re.astype(f32), b_im.astype(f32)
    c_re, c_im = c_re.astype(f32), c_im.astype(f32)
    dt = jnp.exp(log_dt.astype(f32))[:, None]
    mag = jnp.exp(a_re * dt)
    ab_re, ab_im = mag * jnp.cos(a_im * dt), mag * jnp.sin(a_im * dt)
    den = jnp.square(a_re) + jnp.square(a_im)
    f_re = ((ab_re - 1.0) * a_re + ab_im * a_im) / den
    f_im = (ab_im * a_re - (ab_re - 1.0) * a_im) / den
    bb_re = f_re[..., None] * b_re - f_im[..., None] * b_im
    bb_im = f_re[..., None] * b_im + f_im[..., None] * b_re
    uf = u.astype(f32).reshape(bsz, s, SSM_GROUPS, SSM_GROUP_DIM)
    bu_re = jnp.einsum('gph,bsgh->bsgp', bb_re, uf)
    bu_im = jnp.einsum('gph,bsgh->bsgp', bb_im, uf)
    a_sc_re = jnp.broadcast_to(ab_re, bu_re.shape)
    a_sc_im = jnp.broadcast_to(ab_im, bu_re.shape)
    cum_re, cum_im, h_re, h_im = lax.associative_scan(
        _complex_affine_combine, (a_sc_re, a_sc_im, bu_re, bu_im), axis=1)
    h0r = h0_re.astype(f32)[:, None]
    h0i = h0_im.astype(f32)[:, None]
    h_re, h_im = (h_re + cum_re * h0r - cum_im * h0i,
                  h_im + cum_re * h0i + cum_im * h0r)
    y = jnp.einsum('ghp,bsgp->bsgh', c_re, h_re) - jnp.einsum('ghp,bsgp->bsgh', c_im, h_im)
    y = (y + d.astype(f32).reshape(SSM_GROUPS, SSM_GROUP_DIM) * uf).reshape(bsz, s, SSM_WIDTH)
    g = jax.nn.gelu(y)
    out = g * jax.nn.sigmoid(g @ w_glu.astype(f32) + b_glu.astype(f32))
    return out.astype(u.dtype), h_re[:, -1].astype(u.dtype), h_im[:, -1].astype(u.dtype)


def _gmlp(z, ln_g, ln_b, w_s, b_s):
    bsz, s, _ = z.shape
    lc = min(s, CHUNK)
    n = s // lc
    z = jax.nn.gelu(z)
    u, v = z[..., :GMLP_WIDTH], z[..., GMLP_WIDTH:]
    v = _layer_norm(v.reshape(bsz, s, GMLP_HEADS, GMLP_HEAD_DIM), ln_g, ln_b)
    w = jnp.where(jnp.tril(jnp.ones((lc, lc), dtype=bool)), w_s[:, :lc, :lc], 0)
    sv = jnp.einsum('hts,bnshc->bnthc', w, v.reshape(bsz, n, lc, GMLP_HEADS, GMLP_HEAD_DIM))
    sv = sv + b_s[:, :lc].T[None, None, :, :, None]
    out = u * sv.reshape(bsz, s, GMLP_WIDTH)
    return out, v.reshape(bsz, s, GMLP_WIDTH)


def _moe_dispatch(xt, idx, gate, wg, wu, wd):
    t, d = xt.shape
    n_pairs = t * TOP_K
    n_blocks = -(-n_pairs // MOE_BLOCK) + N_EXPERTS
    n_rows = n_blocks * MOE_BLOCK
    flat_e = idx.reshape(-1).astype(jnp.int32)
    flat_t = jnp.repeat(jnp.arange(t, dtype=jnp.int32), TOP_K)
    flat_w = gate.reshape(-1)
    order = jnp.argsort(flat_e)
    se, st, sw = flat_e[order], flat_t[order], flat_w[order]
    counts = jnp.zeros((N_EXPERTS,), jnp.int32).at[flat_e].add(1)
    starts = jnp.cumsum(counts) - counts
    padded = (counts + MOE_BLOCK - 1) // MOE_BLOCK * MOE_BLOCK
    pends = jnp.cumsum(padded)
    dest = (pends - padded)[se] + jnp.arange(n_pairs, dtype=jnp.int32) - starts[se]
    row_tok = jnp.full((n_rows,), t, jnp.int32).at[dest].set(st)
    row_w = jnp.zeros((n_rows,), xt.dtype).at[dest].set(sw)
    block_e = jnp.minimum(
        jnp.searchsorted(pends, jnp.arange(n_blocks, dtype=jnp.int32) * MOE_BLOCK, side='right'),
        N_EXPERTS - 1)
    x_pad = jnp.concatenate([xt, jnp.zeros((1, d), xt.dtype)], axis=0)
    xb = x_pad[row_tok].reshape(n_blocks, MOE_BLOCK, d)

    def expert_block(args):
        xblk, e = args
        h = jax.nn.silu(xblk @ wg[e]) * (xblk @ wu[e])
        return h @ wd[e]

    yb = lax.map(expert_block, (xb, block_e)).reshape(n_rows, d)
    y = jnp.zeros((t + 1, d), xt.dtype).at[row_tok].add(yb * row_w[:, None])
    return y[:t]


def _moe(x, router_w, router_bias, wg, wu, wd, sg, su, sd):
    bsz, s, d = x.shape
    t = bsz * s
    xt = x.reshape(t, d)
    scores = jax.nn.sigmoid((xt @ router_w).astype(jnp.float32))
    sel = scores + router_bias.astype(jnp.float32)
    grp = sel.reshape(t, N_EXPERT_GROUPS, N_EXPERTS // N_EXPERT_GROUPS)
    grp_score = lax.top_k(grp, 2)[0].sum(-1)
    _, top_g = lax.top_k(grp_score, TOPK_GROUPS)
    gmask = jax.nn.one_hot(top_g, N_EXPERT_GROUPS).sum(axis=1) > 0
    emask = jnp.repeat(gmask, N_EXPERTS // N_EXPERT_GROUPS, axis=1)
    _, idx = lax.top_k(jnp.where(emask, sel, -jnp.inf), TOP_K)
    gate = jnp.take_along_axis(scores, idx, axis=1)
    gate = gate / gate.sum(-1, keepdims=True) * ROUTE_SCALE
    routed = _moe_dispatch(xt, idx, gate.astype(x.dtype), wg, wu, wd)
    shared = (jax.nn.silu(xt @ sg) * (xt @ su)) @ sd
    return (routed + shared).reshape(bsz, s, d)


def _layer(x, pos, attend, h0_re, h0_im, p):
    (w_in, q_norm_g, w_uq, kv_norm_g, w_uk, w_uv,
     ssm_log_dt, ssm_a_re, ssm_a_im, ssm_b_re, ssm_b_im, ssm_c_re, ssm_c_im, ssm_d, ssm_w_glu, ssm_b_glu,
     gmlp_ln_g, gmlp_ln_b, gmlp_w_s, gmlp_b_s,
     w_o, ln1_g, ln1_b,
     router_w, router_bias, exp_w_gate, exp_w_up, exp_w_down, sh_w_gate, sh_w_up, sh_w_down,
     ln2_g, ln2_b) = p
    bsz, s, _ = x.shape
    c_q, c_kv, k_r, u_ssm, z_g = jnp.split(x @ w_in, IN_SPLITS, axis=-1)
    q_lat, q_rope, c, kr = _mla_prepare(c_q, c_kv, k_r, pos, q_norm_g, kv_norm_g, w_uq, w_uk)
    o_lat = attend(q_lat, q_rope, c, kr)
    attn = jnp.einsum('bshr,rhv->bshv', o_lat, w_uv).reshape(bsz, s, ATTN_WIDTH)
    ssm, h_re, h_im = _s5(u_ssm, h0_re, h0_im, ssm_log_dt, ssm_a_re, ssm_a_im, ssm_b_re, ssm_b_im,
                          ssm_c_re, ssm_c_im, ssm_d, ssm_w_glu, ssm_b_glu)
    gm, v_rows = _gmlp(z_g, gmlp_ln_g, gmlp_ln_b, gmlp_w_s, gmlp_b_s)
    mix = jnp.concatenate([attn, ssm, gm], axis=-1) @ w_o
    x = _layer_norm(DEEPNORM_ALPHA * x + mix, ln1_g, ln1_b)
    ffn = _moe(x, router_w, router_bias, exp_w_gate, exp_w_up, exp_w_down, sh_w_gate, sh_w_up, sh_w_down)
    x = _layer_norm(DEEPNORM_ALPHA * x + ffn, ln2_g, ln2_b)
    return x, c, kr, h_re, h_im, v_rows


def _trunk(x, pos, attend, h0_re, h0_im, weights):
    per_layer = []
    for l in range(DEPTH):
        p = tuple(w[l] for w in weights)
        x, c, kr, h_re, h_im, v_rows = _layer(x, pos, functools.partial(attend, l), h0_re[l], h0_im[l], p)
        per_layer.append((c, kr, h_re, h_im, v_rows))
    stacked = [jnp.stack(st, axis=0) for st in zip(*per_layer)]
    return x, stacked


def setup_inputs(seed: int = 0) -> dict:
    key = jax.random.key(seed)
    ks = jax.random.split(key, 64)
    counter = iter(range(64))
    f32 = jnp.float32

    def nk():
        return ks[next(counter)]

    def nrm(shape, scale=1.0):
        return scale * jax.random.normal(nk(), shape, f32)

    def gain(shape):
        return 1.0 + nrm(shape, 0.02)

    n_pages = PAST_LEN // PAGE_SIZE
    n_used = DEC_BATCH * n_pages
    n_phys = n_used + n_used // 4
    page_table = jax.random.permutation(nk(), n_phys)[:n_used].reshape(DEC_BATCH, n_pages).astype(jnp.int32)
    beta = DEEPNORM_BETA
    a_re = -0.5 + nrm((DEPTH, SSM_GROUPS, SSM_STATE), 0.01)
    a_im = math.pi * jnp.arange(SSM_STATE, dtype=f32)[None, None, :] + nrm((DEPTH, SSM_GROUPS, SSM_STATE), 0.01)
    log_dt = jax.random.uniform(nk(), (DEPTH, SSM_GROUPS), f32, math.log(DT_MIN), math.log(DT_MAX))
    return {
        'x_prompt': nrm((BATCH, SEQ, D_MODEL)),
        'x_sample': nrm((DEC_BATCH, DEC_SEQ, D_MODEL)),
        'cache_kv': nrm((DEPTH, n_phys, PAGE_SIZE, KV_RANK)),
        'cache_kr': nrm((DEPTH, n_phys, PAGE_SIZE, ROPE_DIM)),
        'state_ssm_re': nrm((DEPTH, DEC_BATCH, SSM_GROUPS, SSM_STATE), 0.3),
        'state_ssm_im': nrm((DEPTH, DEC_BATCH, SSM_GROUPS, SSM_STATE), 0.3),
        'page_table': page_table,
        'w_in': nrm((DEPTH, D_MODEL, IN_WIDTH), D_MODEL ** -0.5),
        'q_norm_g': gain((DEPTH, Q_RANK)),
        'w_uq': nrm((DEPTH, Q_RANK, MLA_HEADS * (NOPE_DIM + ROPE_DIM)), Q_RANK ** -0.5),
        'kv_norm_g': gain((DEPTH, KV_RANK)),
        'w_uk': nrm((DEPTH, KV_RANK, MLA_HEADS, NOPE_DIM), KV_RANK ** -0.5),
        'w_uv': nrm((DEPTH, KV_RANK, MLA_HEADS, V_DIM), beta * KV_RANK ** -0.5),
        'ssm_log_dt': log_dt,
        'ssm_a_re': a_re,
        'ssm_a_im': a_im,
        'ssm_b_re': nrm((DEPTH, SSM_GROUPS, SSM_STATE, SSM_GROUP_DIM), (2.0 * SSM_GROUP_DIM) ** -0.5),
        'ssm_b_im': nrm((DEPTH, SSM_GROUPS, SSM_STATE, SSM_GROUP_DIM), (2.0 * SSM_GROUP_DIM) ** -0.5),
        'ssm_c_re': nrm((DEPTH, SSM_GROUPS, SSM_GROUP_DIM, SSM_STATE), (2.0 * SSM_STATE) ** -0.5),
        'ssm_c_im': nrm((DEPTH, SSM_GROUPS, SSM_GROUP_DIM, SSM_STATE), (2.0 * SSM_STATE) ** -0.5),
        'ssm_d': nrm((DEPTH, SSM_WIDTH)),
        'ssm_w_glu': nrm((DEPTH, SSM_WIDTH, SSM_WIDTH), SSM_WIDTH ** -0.5),
        'ssm_b_glu': nrm((DEPTH, SSM_WIDTH), 0.01),
        'gmlp_ln_g': gain((DEPTH, GMLP_HEADS, GMLP_HEAD_DIM)),
        'gmlp_ln_b': nrm((DEPTH, GMLP_HEADS, GMLP_HEAD_DIM), 0.01),
        'gmlp_w_s': nrm((DEPTH, GMLP_HEADS, CHUNK, CHUNK), CHUNK ** -0.5),
        'gmlp_b_s': 1.0 + nrm((DEPTH, GMLP_HEADS, CHUNK), 0.1),
        'w_o': nrm((DEPTH, MIX_WIDTH, D_MODEL), beta * MIX_WIDTH ** -0.5),
        'ln1_g': gain((DEPTH, D_MODEL)),
        'ln1_b': nrm((DEPTH, D_MODEL), 0.01),
        'router_w': nrm((DEPTH, D_MODEL, N_EXPERTS), D_MODEL ** -0.5),
        'router_bias': nrm((DEPTH, N_EXPERTS), 0.01),
        'exp_w_gate': nrm((DEPTH, N_EXPERTS, D_MODEL, EXPERT_FF), D_MODEL ** -0.5),
        'exp_w_up': nrm((DEPTH, N_EXPERTS, D_MODEL, EXPERT_FF), D_MODEL ** -0.5),
        'exp_w_down': nrm((DEPTH, N_EXPERTS, EXPERT_FF, D_MODEL), beta * EXPERT_FF ** -0.5),
        'sh_w_gate': nrm((DEPTH, D_MODEL, SHARED_FF), D_MODEL ** -0.5),
        'sh_w_up': nrm((DEPTH, D_MODEL, SHARED_FF), D_MODEL ** -0.5),
        'sh_w_down': nrm((DEPTH, SHARED_FF, D_MODEL), beta * SHARED_FF ** -0.5),
        'ln2_g': gain((DEPTH, D_MODEL)),
        'ln2_b': nrm((DEPTH, D_MODEL), 0.01),
    }


def reference(x_prompt, x_sample, cache_kv, cache_kr, state_ssm_re, state_ssm_im, page_table,
              w_in, q_norm_g, w_uq, kv_norm_g, w_uk, w_uv,
              ssm_log_dt, ssm_a_re, ssm_a_im, ssm_b_re, ssm_b_im, ssm_c_re, ssm_c_im, ssm_d,
              ssm_w_glu, ssm_b_glu,
              gmlp_ln_g, gmlp_ln_b, gmlp_w_s, gmlp_b_s,
              w_o, ln1_g, ln1_b,
              router_w, router_bias, exp_w_gate, exp_w_up, exp_w_down, sh_w_gate, sh_w_up, sh_w_down,
              ln2_g, ln2_b):
    weights = (w_in, q_norm_g, w_uq, kv_norm_g, w_uk, w_uv,
               ssm_log_dt, ssm_a_re, ssm_a_im, ssm_b_re, ssm_b_im, ssm_c_re, ssm_c_im, ssm_d,
               ssm_w_glu, ssm_b_glu,
               gmlp_ln_g, gmlp_ln_b, gmlp_w_s, gmlp_b_s,
               w_o, ln1_g, ln1_b,
               router_w, router_bias, exp_w_gate, exp_w_up, exp_w_down, sh_w_gate, sh_w_up, sh_w_down,
               ln2_g, ln2_b)
    dec_b = x_sample.shape[0]

    def attend_prompt(l, q_lat, q_rope, c, kr):
        return _mla_attend_self(q_lat, q_rope, c, kr)

    def attend_sample(l, q_lat, q_rope, c, kr):
        c_past = cache_kv[l, page_table].reshape(dec_b, -1, KV_RANK)
        kr_past = cache_kr[l, page_table].reshape(dec_b, -1, ROPE_DIM)
        return _mla_attend_cached(q_lat, q_rope, c, kr, c_past, kr_past)

    pos_p = jnp.arange(x_prompt.shape[1], dtype=jnp.int32)
    pos_s = PAST_LEN + jnp.arange(x_sample.shape[1], dtype=jnp.int32)
    h0_p = jnp.zeros((DEPTH, x_prompt.shape[0], SSM_GROUPS, SSM_STATE), jnp.float32)
    y_prompt, st_p = _trunk(x_prompt, pos_p, attend_prompt, h0_p, h0_p, weights)
    y_sample, st_s = _trunk(x_sample, pos_s, attend_sample, state_ssm_re, state_ssm_im, weights)
    kv_p, kr_p, ssm_re_p, ssm_im_p, _v_p = st_p
    kv_s, kr_s, ssm_re_s, ssm_im_s, v_s = st_s
    return (y_prompt, y_sample, kv_p, kr_p, ssm_re_p, ssm_im_p, kv_s, kr_s, ssm_re_s, ssm_im_s, v_s)
```

```python
import functools
import math

import jax
import jax.numpy as jnp
from jax import lax
from jax.experimental import pallas as pl
from jax.experimental.pallas import tpu as pltpu

F32 = jnp.float32
BF16 = jnp.bfloat16
I32 = jnp.int32
U32 = jnp.uint32

LANES_V7X = 128
VMEM_LIMIT_V7X = 56 * 1024 * 1024

ROPE_THETA = 10000.0
EPS = 1e-6
TOP_K = 8
N_EXPERT_GROUPS = 8
TOPK_GROUPS = 4
ROUTE_SCALE = 2.5

ROW_TILE = 256
MOE_ROWS = 128
MOE_TOKENS = 128
Q_TILE = 128
KV_TILE = 256
PAGE_GROUP = 8
QK_PAD = 384


def _params(*sem):
    return pltpu.CompilerParams(dimension_semantics=sem, vmem_limit_bytes=VMEM_LIMIT_V7X)


def _gelu(x):
    c = math.sqrt(2.0 / math.pi)
    return x * (0.5 * (1.0 + jnp.tanh(c * (x + 0.044715 * (x * x * x)))))


def _rms(x, g):
    return x * lax.rsqrt(jnp.mean(x * x, axis=-1, keepdims=True) + EPS) * g


def _layer_norm(x, g, b):
    mu = jnp.mean(x, axis=-1, keepdims=True)
    xc = x - mu
    var = jnp.mean(xc * xc, axis=-1, keepdims=True)
    return xc * lax.rsqrt(var + EPS) * g + b


def _dot(a, b):
    return jnp.dot(a, b, preferred_element_type=F32)


def _dot_nt(a, b):
    return lax.dot_general(a, b, (((1,), (1,)), ((), ())), preferred_element_type=F32)


def _in_proj_kernel(x_ref, wq_ref, wkv_ref, wkr_ref, wu_ref, wz_ref, gq_ref, gkv_ref, cos_ref, sin_ref,
                    lng_ref, lnb_ref, mix_ref, mixb_ref,
                    cq_ref, c_ref, kr_ref, kcat_ref, u_ref, gm_ref, v_ref, *, kv_rank, rope, g_heads, g_dim, chunk):
    xb = x_ref[...].astype(BF16)
    cq_ref[...] = _rms(_dot(xb, wq_ref[...]), gq_ref[...]).astype(BF16)
    c = _rms(_dot(xb, wkv_ref[...]), gkv_ref[...])
    c_ref[...] = c
    kr2 = _dot(xb, wkr_ref[...])
    kr = kr2[:, :rope] * cos_ref[...] + kr2[:, rope:] * sin_ref[...]
    kr_ref[...] = kr
    kcat_ref[:, :kv_rank] = c.astype(BF16)
    kcat_ref[:, kv_rank:kv_rank + rope] = kr.astype(BF16)
    kcat_ref[:, kv_rank + rope:] = jnp.zeros((kcat_ref.shape[0], kcat_ref.shape[1] - kv_rank - rope), BF16)
    u_ref[...] = _dot(xb, wu_ref[...])
    z = _gelu(_dot(xb, wz_ref[...]))
    width = g_heads * g_dim
    for h in range(g_heads):
        sl = slice(h * g_dim, (h + 1) * g_dim)
        vh = _layer_norm(z[:, width + h * g_dim: width + (h + 1) * g_dim], lng_ref[:, sl], lnb_ref[:, sl])
        v_ref[:, sl] = vh
        vb = vh.astype(BF16)
        for j in range(x_ref.shape[0] // chunk):
            rows = slice(j * chunk, (j + 1) * chunk)
            sv = _dot(mix_ref[h], vb[rows]) + mixb_ref[:, sl]
            gm_ref[rows, sl] = (z[rows, sl] * sv).astype(BF16)


def _in_proj(x, lw, tabs, n_prompt_tiles):
    t, d = x.shape
    tm = ROW_TILE
    kv_rank, rope = lw["wkv"].shape[1], lw["wkr"].shape[1] // 2
    gw = lw["lng"].shape[1]
    g_heads, chunk = lw["mix"].shape[1], lw["mix"].shape[2]
    row = lambda i: (i, 0)
    full = lambda i: (0, 0)
    trunk = lambda i: (jnp.where(i >= n_prompt_tiles, 1, 0), 0, 0, 0)
    trunk3 = lambda i: (jnp.where(i >= n_prompt_tiles, 1, 0), 0, 0)
    kern = functools.partial(_in_proj_kernel, kv_rank=kv_rank, rope=rope, g_heads=g_heads, g_dim=gw // g_heads,
                             chunk=chunk)
    q_rank, ssm_w = lw["wq"].shape[1], lw["wu"].shape[1]
    return pl.pallas_call(
        kern,
        grid=(t // tm,),
        in_specs=[
            pl.BlockSpec((tm, d), row),
            pl.BlockSpec(lw["wq"].shape, full), pl.BlockSpec(lw["wkv"].shape, full),
            pl.BlockSpec(lw["wkr"].shape, full), pl.BlockSpec(lw["wu"].shape, full),
            pl.BlockSpec(lw["wz"].shape, full),
            pl.BlockSpec(lw["gq"].shape, full), pl.BlockSpec(lw["gkv"].shape, full),
            pl.BlockSpec((tm, rope), row), pl.BlockSpec((tm, rope), row),
            pl.BlockSpec(lw["lng"].shape, full), pl.BlockSpec(lw["lnb"].shape, full),
            pl.BlockSpec((None,) + lw["mix"].shape[1:], trunk),
            pl.BlockSpec((None,) + lw["mixb"].shape[1:], trunk3),
        ],
        out_specs=[
            pl.BlockSpec((tm, q_rank), row), pl.BlockSpec((tm, kv_rank), row), pl.BlockSpec((tm, rope), row),
            pl.BlockSpec((tm, QK_PAD), row), pl.BlockSpec((tm, ssm_w), row), pl.BlockSpec((tm, gw), row),
            pl.BlockSpec((tm, gw), row),
        ],
        out_shape=[
            jax.ShapeDtypeStruct((t, q_rank), BF16), jax.ShapeDtypeStruct((t, kv_rank), F32),
            jax.ShapeDtypeStruct((t, rope), F32), jax.ShapeDtypeStruct((t, QK_PAD), BF16),
            jax.ShapeDtypeStruct((t, ssm_w), F32), jax.ShapeDtypeStruct((t, gw), BF16),
            jax.ShapeDtypeStruct((t, gw), F32),
        ],
        compiler_params=_params("parallel"),
        name="in_proj",
    )(x, lw["wq"], lw["wkv"], lw["wkr"], lw["wu"], lw["wz"], lw["gq"], lw["gkv"], tabs["cos"], tabs["sin"],
      lw["lng"], lw["lnb"], lw["mix"], lw["mixb"])


def _mla_q_kernel(cq_ref, wqn_ref, wqr_ref, wuk_ref, cos_ref, sin_ref, q_ref, *, kv_rank, rope):
    cq = cq_ref[...]
    for h in range(q_ref.shape[0]):
        qn = _dot(cq, wqn_ref[h]).astype(BF16)
        q_ref[h, :, :kv_rank] = _dot(qn, wuk_ref[h]).astype(q_ref.dtype)
        qr2 = _dot(cq, wqr_ref[h])
        qr = qr2[:, :rope] * cos_ref[...] + qr2[:, rope:] * sin_ref[...]
        q_ref[h, :, kv_rank:kv_rank + rope] = qr.astype(q_ref.dtype)
        q_ref[h, :, kv_rank + rope:] = jnp.zeros((q_ref.shape[1], q_ref.shape[2] - kv_rank - rope), q_ref.dtype)


def _mla_q(cq, lw, tabs, row0, n_rows, out_dtype):
    tm = ROW_TILE
    heads, q_rank, _ = lw["wqn"].shape
    kv_rank, rope = lw["wuk"].shape[2], lw["wqr"].shape[2] // 2
    off = row0 // tm
    row = lambda i: (i + off, 0)
    full = lambda i: (0, 0, 0)
    return pl.pallas_call(
        functools.partial(_mla_q_kernel, kv_rank=kv_rank, rope=rope),
        grid=(n_rows // tm,),
        in_specs=[
            pl.BlockSpec((tm, q_rank), row),
            pl.BlockSpec(lw["wqn"].shape, full), pl.BlockSpec(lw["wqr"].shape, full),
            pl.BlockSpec(lw["wuk"].shape, full),
            pl.BlockSpec((tm, rope), row), pl.BlockSpec((tm, rope), row),
        ],
        out_specs=pl.BlockSpec((heads, tm, QK_PAD), lambda i: (0, i, 0)),
        out_shape=jax.ShapeDtypeStruct((heads, n_rows, QK_PAD), out_dtype),
        compiler_params=_params("parallel"),
        name="mla_q",
    )(cq, lw["wqn"], lw["wqr"], lw["wuk"], tabs["cos"], tabs["sin"])


def _softmax_step(s, v, m_scr, l_scr, acc_scr):
    m_prev = m_scr[...]
    m_new = jnp.maximum(m_prev, jnp.max(s, axis=-1, keepdims=True))
    alpha = jnp.exp(m_prev - m_new)
    p = jnp.exp(s - m_new[:, :1])
    l_scr[...] = alpha * l_scr[...] + jnp.sum(p, axis=-1, keepdims=True)
    acc_scr[...] = alpha[:, :1] * acc_scr[...] + _dot(p.astype(BF16), v)
    m_scr[...] = m_new


def _attn_prompt_kernel(q_ref, k_ref, wuv_ref, o_ref, m_scr, l_scr, acc_scr, *, tq, tk, kv_rank, scale):
    qi = pl.program_id(1)
    heads = q_ref.shape[0]
    q = q_ref[...].reshape(heads * tq, q_ref.shape[2])
    m_scr[...] = jnp.full(m_scr.shape, -jnp.inf, F32)
    l_scr[...] = jnp.zeros(l_scr.shape, F32)
    acc_scr[...] = jnp.zeros(acc_scr.shape, F32)
    n_full = (qi * tq) // tk

    def full_block(j, carry):
        kb = k_ref[pl.ds(pl.multiple_of(j * tk, tk), tk), :]
        _softmax_step(_dot_nt(q, kb) * scale, kb[:, :kv_rank], m_scr, l_scr, acc_scr)
        return carry

    lax.fori_loop(0, n_full, full_block, 0)
    kb = k_ref[pl.ds(pl.multiple_of(n_full * tk, tk), tk), :]
    s = _dot_nt(q, kb) * scale
    qpos = qi * tq + lax.broadcasted_iota(I32, s.shape, 0) % tq
    kpos = n_full * tk + lax.broadcasted_iota(I32, s.shape, 1)
    _softmax_step(jnp.where(kpos <= qpos, s, -jnp.inf), kb[:, :kv_rank], m_scr, l_scr, acc_scr)
    o = (acc_scr[...] / l_scr[:, :1]).astype(BF16)
    v_dim = wuv_ref.shape[2]
    for h in range(heads):
        o_ref[:, h * v_dim:(h + 1) * v_dim] = _dot(o[h * tq:(h + 1) * tq], wuv_ref[h])


def _attn_prompt(q, kcat, wuv, batch, seq, scale):
    heads, tp, _ = q.shape
    kv_rank, v_dim = wuv.shape[1], wuv.shape[2]
    tq, tk = Q_TILE, KV_TILE
    nq = seq // tq
    rows = heads * tq
    return pl.pallas_call(
        functools.partial(_attn_prompt_kernel, tq=tq, tk=tk, kv_rank=kv_rank, scale=scale),
        grid=(batch, nq),
        in_specs=[
            pl.BlockSpec((heads, tq, QK_PAD), lambda b, i: (0, b * nq + i, 0)),
            pl.BlockSpec((seq, QK_PAD), lambda b, i: (b, 0)),
            pl.BlockSpec(wuv.shape, lambda b, i: (0, 0, 0)),
        ],
        out_specs=pl.BlockSpec((tq, heads * v_dim), lambda b, i: (b * nq + i, 0)),
        out_shape=jax.ShapeDtypeStruct((tp, heads * v_dim), F32),
        scratch_shapes=[pltpu.VMEM((rows, LANES_V7X), F32), pltpu.VMEM((rows, LANES_V7X), F32),
                        pltpu.VMEM((rows, kv_rank), F32)],
        compiler_params=_params("parallel", "arbitrary"),
        name="attn_prompt",
    )(q, kcat, wuv)


def _attn_sample_kernel(pt_ref, q_ref, cnew_ref, krnew_ref, wuv_ref, ckv_hbm, ckr_hbm, o_ref,
                        kvbuf, krbuf, kb_scr, knew_scr, sems, m_scr, l_scr, acc_scr,
                        *, layer, n_pages, group, page, kv_rank, rope, scale):
    b = pl.program_id(0)
    heads, s_new, _ = q_ref.shape
    rows = heads * s_new
    n_groups = n_pages // group
    q = q_ref[...].reshape(rows, q_ref.shape[2]).astype(BF16)

    def copies(g, slot, p):
        pg = pt_ref[b * n_pages + g * group + p]
        return (pltpu.make_async_copy(ckv_hbm.at[layer, pg], kvbuf.at[slot, p], sems.at[0, slot]),
                pltpu.make_async_copy(ckr_hbm.at[layer, pg], krbuf.at[slot, p], sems.at[1, slot]))

    def start(g, slot):
        for p in range(group):
            for cp in copies(g, slot, p):
                cp.start()

    def wait(g, slot):
        for p in range(group):
            for cp in copies(g, slot, p):
                cp.wait()

    m_scr[...] = jnp.full(m_scr.shape, -jnp.inf, F32)
    l_scr[...] = jnp.zeros(l_scr.shape, F32)
    acc_scr[...] = jnp.zeros(acc_scr.shape, F32)
    pad = kb_scr.shape[1] - kv_rank - rope
    kb_scr[:, kv_rank + rope:] = jnp.zeros((kb_scr.shape[0], pad), BF16)
    start(0, 0)

    def body(g, carry):
        slot = g % 2

        @pl.when(g + 1 < n_groups)
        def _():
            start(g + 1, 1 - slot)

        wait(g, slot)
        kb_scr[:, :kv_rank] = kvbuf[slot].reshape(group * page, kv_rank).astype(BF16)
        kb_scr[:, kv_rank:kv_rank + rope] = krbuf[slot].reshape(group * page, rope).astype(BF16)
        kb = kb_scr[...]
        _softmax_step(_dot_nt(q, kb) * scale, kb[:, :kv_rank], m_scr, l_scr, acc_scr)
        return carry

    lax.fori_loop(0, n_groups, body, 0)

    knew_scr[...] = jnp.zeros(knew_scr.shape, BF16)
    knew_scr[:s_new, :kv_rank] = cnew_ref[...].astype(BF16)
    knew_scr[:s_new, kv_rank:kv_rank + rope] = krnew_ref[...].astype(BF16)
    kn = knew_scr[...]
    s = _dot_nt(q, kn) * scale
    qpos = lax.broadcasted_iota(I32, s.shape, 0) % s_new
    kpos = lax.broadcasted_iota(I32, s.shape, 1)
    _softmax_step(jnp.where(kpos <= qpos, s, -jnp.inf), kn[:, :kv_rank], m_scr, l_scr, acc_scr)
    o = (acc_scr[...] / l_scr[:, :1]).astype(BF16)
    v_dim = wuv_ref.shape[2]
    for h in range(heads):
        o_ref[:, h * v_dim:(h + 1) * v_dim] = _dot(o, wuv_ref[h])[h * s_new:(h + 1) * s_new]


def _attn_sample(q, c, kr, row0, wuv, cache_kv, cache_kr, page_table, layer, scale):
    heads, ts, _ = q.shape
    dec_b, n_pages = page_table.shape
    s_new = ts // dec_b
    page, kv_rank = cache_kv.shape[2], cache_kv.shape[3]
    rope = cache_kr.shape[3]
    v_dim = wuv.shape[2]
    group = PAGE_GROUP
    rows = heads * s_new
    off = row0 // s_new
    grid_spec = pltpu.PrefetchScalarGridSpec(
        num_scalar_prefetch=1,
        grid=(dec_b,),
        in_specs=[
            pl.BlockSpec((heads, s_new, QK_PAD), lambda b, pt: (0, b, 0)),
            pl.BlockSpec((s_new, kv_rank), lambda b, pt: (b + off, 0)),
            pl.BlockSpec((s_new, rope), lambda b, pt: (b + off, 0)),
            pl.BlockSpec(wuv.shape, lambda b, pt: (0, 0, 0)),
            pl.BlockSpec(memory_space=pl.ANY),
            pl.BlockSpec(memory_space=pl.ANY),
        ],
        out_specs=pl.BlockSpec((s_new, heads * v_dim), lambda b, pt: (b, 0)),
        scratch_shapes=[
            pltpu.VMEM((2, group, page, kv_rank), F32),
            pltpu.VMEM((2, group, page, rope), F32),
            pltpu.VMEM((group * page, QK_PAD), BF16),
            pltpu.VMEM((LANES_V7X, QK_PAD), BF16),
            pltpu.SemaphoreType.DMA((2, 2)),
            pltpu.VMEM((rows, LANES_V7X), F32), pltpu.VMEM((rows, LANES_V7X), F32),
            pltpu.VMEM((rows, kv_rank), F32),
        ],
    )
    return pl.pallas_call(
        functools.partial(_attn_sample_kernel, layer=layer, n_pages=n_pages, group=group, page=page,
                          kv_rank=kv_rank, rope=rope, scale=scale),
        grid_spec=grid_spec,
        out_shape=jax.ShapeDtypeStruct((ts, heads * v_dim), F32),
        compiler_params=_params("arbitrary"),
        name="attn_sample",
    )(page_table.reshape(-1), q, c, kr, wuv, cache_kv, cache_kr)


def _s5_kernel(u_ref, h0_ref, wbu_ref, a_ref, wc_ref, d_ref, wglu_ref, bglu_ref, y_ref, ht_ref, hs_scr, carry_scr,
               *, seq_rows, seqs_per_tile, tiles_per_seq):
    i = pl.program_id(0)
    u = u_ref[...]
    hs_scr[...] = _dot(u.astype(BF16), wbu_ref[...])
    n = a_ref.shape[1]
    a_re, a_im = a_ref[0:1, :], a_ref[1:2, :]

    if tiles_per_seq > 1:
        @pl.when(i % tiles_per_seq == 0)
        def _():
            carry_scr[...] = h0_ref[0]

    def one_sequence(s, carry):
        h = carry_scr[...] if tiles_per_seq > 1 else h0_ref[s]

        def row(t, hc):
            hr, hi = hc
            r = s * seq_rows + t
            nr = a_re * hr - a_im * hi + hs_scr[pl.ds(r, 1), :n]
            ni = a_re * hi + a_im * hr + hs_scr[pl.ds(r, 1), n:]
            hs_scr[pl.ds(r, 1), :n] = nr
            hs_scr[pl.ds(r, 1), n:] = ni
            return nr, ni

        hr, hi = lax.fori_loop(0, seq_rows, row, (h[:, :n], h[:, n:]), unroll=8)
        last = jnp.concatenate([hr, hi], axis=1)
        ht_ref[s] = last
        if tiles_per_seq > 1:
            carry_scr[...] = last
        return carry

    lax.fori_loop(0, seqs_per_tile, one_sequence, 0)
    y = _dot(hs_scr[...].astype(BF16), wc_ref[...]) + d_ref[...] * u
    g = _gelu(y)
    y_ref[...] = g * jax.nn.sigmoid(_dot(g.astype(BF16), wglu_ref[...]) + bglu_ref[...])


def _s5(u, h0, lw, row0, n_rows, seq_len):
    tc = ROW_TILE
    width = u.shape[1]
    n2 = lw["wbu"].shape[1]
    seq_rows = min(seq_len, tc)
    seqs_per_tile = tc // seq_rows
    tiles_per_seq = max(1, seq_len // tc)
    n_seq = n_rows // seq_len
    off = row0 // tc
    full = lambda i: (0, 0)
    seq_idx = (lambda i: (i // tiles_per_seq, 0, 0)) if tiles_per_seq > 1 else (lambda i: (i, 0, 0))
    return pl.pallas_call(
        functools.partial(_s5_kernel, seq_rows=seq_rows, seqs_per_tile=seqs_per_tile, tiles_per_seq=tiles_per_seq),
        grid=(n_rows // tc,),
        in_specs=[
            pl.BlockSpec((tc, width), lambda i: (i + off, 0)),
            pl.BlockSpec((seqs_per_tile, 1, n2), seq_idx),
            pl.BlockSpec(lw["wbu"].shape, full), pl.BlockSpec(lw["a"].shape, full),
            pl.BlockSpec(lw["wc"].shape, full), pl.BlockSpec(lw["d"].shape, full),
            pl.BlockSpec(lw["wglu"].shape, full), pl.BlockSpec(lw["bglu"].shape, full),
        ],
        out_specs=[pl.BlockSpec((tc, width), lambda i: (i, 0)), pl.BlockSpec((seqs_per_tile, 1, n2), seq_idx)],
        out_shape=[jax.ShapeDtypeStruct((n_rows, width), F32), jax.ShapeDtypeStruct((n_seq, 1, n2), F32)],
        scratch_shapes=[pltpu.VMEM((tc, n2), F32), pltpu.VMEM((1, n2), F32)],
        compiler_params=_params("arbitrary"),
        name="s5",
    )(u, h0, lw["wbu"], lw["a"], lw["wc"], lw["d"], lw["wglu"], lw["bglu"])


def _pack_bf16_pairs(x):
    half = x.shape[1] // 2
    hi = pltpu.bitcast(x[:, :half].astype(BF16).astype(F32), U32)
    lo = pltpu.bitcast(x[:, half:].astype(BF16).astype(F32), U32)
    return hi | (lo >> 16)


def _unpack_bf16_pairs(p):
    hi = pltpu.bitcast(p & jnp.uint32(0xFFFF0000), F32).astype(BF16)
    lo = pltpu.bitcast(p << 16, F32).astype(BF16)
    return hi, lo


def _out_proj_kernel(x_ref, a_ref, s_ref, g_ref, woa_ref, wos_ref, wog_ref, lg_ref, lb_ref, x1_ref, xp_ref, *, alpha):
    mix = (_dot(a_ref[...].astype(BF16), woa_ref[...]) + _dot(s_ref[...].astype(BF16), wos_ref[...])
           + _dot(g_ref[...], wog_ref[...]))
    x1 = _layer_norm(alpha * x_ref[...] + mix, lg_ref[...], lb_ref[...])
    x1_ref[...] = x1
    xp_ref[...] = _pack_bf16_pairs(x1)


def _out_proj(x, attn, ssm, gm, lw, alpha):
    t, d = x.shape
    tm = ROW_TILE
    row = lambda i: (i, 0)
    full = lambda i: (0, 0)
    return pl.pallas_call(
        functools.partial(_out_proj_kernel, alpha=alpha),
        grid=(t // tm,),
        in_specs=[
            pl.BlockSpec((tm, d), row), pl.BlockSpec((tm, attn.shape[1]), row), pl.BlockSpec((tm, ssm.shape[1]), row),
            pl.BlockSpec((tm, gm.shape[1]), row),
            pl.BlockSpec(lw["woa"].shape, full), pl.BlockSpec(lw["wos"].shape, full), pl.BlockSpec(lw["wog"].shape, full),
            pl.BlockSpec(lw["ln1g"].shape, full), pl.BlockSpec(lw["ln1b"].shape, full),
        ],
        out_specs=[pl.BlockSpec((tm, d), row), pl.BlockSpec((tm, d // 2), row)],
        out_shape=[jax.ShapeDtypeStruct((t, d), F32), jax.ShapeDtypeStruct((t, d // 2), U32)],
        compiler_params=_params("parallel"),
        name="out_proj",
    )(x, attn, ssm, gm, lw["woa"], lw["wos"], lw["wog"], lw["ln1g"], lw["ln1b"])


def _first_index_of_max(x, iota, n):
    m = jnp.max(x, axis=0, keepdims=True)
    return m, jnp.min(jnp.where(x == m, iota, n), axis=0, keepdims=True)


def _router_kernel(x_ref, rwt_ref, bias_ref, idx_ref, gate_ref, rank_ref, cnt_ref, carry_scr):
    i = pl.program_id(0)

    @pl.when(i == 0)
    def _():
        carry_scr[...] = jnp.zeros(carry_scr.shape, F32)

    n_e = rwt_ref.shape[0]
    tm = x_ref.shape[0]
    per_group = n_e // N_EXPERT_GROUPS
    logits = lax.dot_general(rwt_ref[...], x_ref[...], (((1,), (1,)), ((), ())),
                             precision=lax.Precision.HIGHEST, preferred_element_type=F32)
    scores = jax.nn.sigmoid(logits)
    sel = scores + bias_ref[...]
    neg = -jnp.inf
    sub = lax.broadcasted_iota(I32, (per_group, tm), 0)
    group_scores = []
    for g in range(N_EXPERT_GROUPS):
        blk = sel[g * per_group:(g + 1) * per_group, :]
        m1, i1 = _first_index_of_max(blk, sub, per_group)
        m2 = jnp.max(jnp.where(sub == i1, neg, blk), axis=0, keepdims=True)
        group_scores.append(m1 + m2)
    work = jnp.concatenate(group_scores, axis=0)
    g_iota = lax.broadcasted_iota(I32, work.shape, 0)
    keep = jnp.zeros(work.shape, F32)
    for _ in range(TOPK_GROUPS):
        _, gi = _first_index_of_max(work, g_iota, N_EXPERT_GROUPS)
        hit = g_iota == gi
        keep = jnp.where(hit, 1.0, keep)
        work = jnp.where(hit, neg, work)
    keep_e = jnp.concatenate([jnp.broadcast_to(keep[g:g + 1, :], (per_group, tm)) for g in range(N_EXPERT_GROUPS)], axis=0)
    cand = jnp.where(keep_e > 0.0, sel, neg)
    e_iota = lax.broadcasted_iota(I32, (n_e, tm), 0)
    onehot = jnp.zeros((n_e, tm), F32)
    idx_rows, gate_rows = [], []
    for _ in range(TOP_K):
        _, ik = _first_index_of_max(cand, e_iota, n_e)
        hit = e_iota == ik
        gate_rows.append(jnp.sum(jnp.where(hit, scores, 0.0), axis=0, keepdims=True))
        onehot = jnp.where(hit, 1.0, onehot)
        cand = jnp.where(hit, neg, cand)
        idx_rows.append(ik)
    gates = jnp.concatenate(gate_rows, axis=0)
    gate_ref[...] = gates / jnp.sum(gates, axis=0, keepdims=True) * ROUTE_SCALE
    idx_ref[...] = jnp.concatenate(idx_rows, axis=0)
    before = (lax.broadcasted_iota(I32, (tm, tm), 0) < lax.broadcasted_iota(I32, (tm, tm), 1)).astype(BF16)
    prefix = _dot(onehot.astype(BF16), before) + carry_scr[:, :1]
    rank_rows = [jnp.sum(jnp.where(e_iota == ik, prefix, 0.0), axis=0, keepdims=True) for ik in idx_rows]
    rank_ref[...] = jnp.concatenate(rank_rows, axis=0).astype(I32)
    carry_scr[...] = carry_scr[...] + jnp.sum(onehot, axis=1, keepdims=True)
    cnt_ref[...] = carry_scr[...]


def _router(x1, lw):
    t, d = x1.shape
    tm = ROW_TILE
    n_e = lw["rwt"].shape[0]
    col = lambda i: (0, i)
    full = lambda i: (0, 0)
    return pl.pallas_call(
        _router_kernel,
        grid=(t // tm,),
        in_specs=[pl.BlockSpec((tm, d), lambda i: (i, 0)), pl.BlockSpec(lw["rwt"].shape, full),
                  pl.BlockSpec(lw["rbias"].shape, full)],
        out_specs=[pl.BlockSpec((TOP_K, tm), col), pl.BlockSpec((TOP_K, tm), col), pl.BlockSpec((TOP_K, tm), col),
                   pl.BlockSpec((n_e, LANES_V7X), full)],
        out_shape=[jax.ShapeDtypeStruct((TOP_K, t), I32), jax.ShapeDtypeStruct((TOP_K, t), F32),
                   jax.ShapeDtypeStruct((TOP_K, t), I32), jax.ShapeDtypeStruct((n_e, LANES_V7X), F32)],
        scratch_shapes=[pltpu.VMEM((n_e, LANES_V7X), F32)],
        compiler_params=_params("arbitrary"),
        name="router",
    )(x1, lw["rwt"], lw["rbias"])


def _dispatch_kernel(dest_ref, xp_ref, xs_in, xs_hbm, sem):
    del xs_in
    tokens = xp_ref.shape[0]

    def copy(tok, k):
        d = dest_ref[k, tok]
        return pltpu.make_async_copy(xp_ref.at[pl.ds(tok, 1), :], xs_hbm.at[pl.ds(d, 1), :], sem)

    def start(tok, carry):
        for k in range(TOP_K):
            copy(tok, k).start()
        return carry

    def wait(tok, carry):
        for k in range(TOP_K):
            copy(tok, k).wait()
        return carry

    lax.fori_loop(0, tokens, start, 0)
    lax.fori_loop(0, tokens, wait, 0)


def _dispatch(xp, dest, n_rows):
    t, w = xp.shape
    tm = MOE_TOKENS
    xs0 = jnp.zeros((n_rows, w), U32)
    return pl.pallas_call(
        _dispatch_kernel,
        grid=(t // tm,),
        in_specs=[pl.BlockSpec((TOP_K, tm), lambda i: (0, i), memory_space=pltpu.SMEM),
                  pl.BlockSpec((tm, w), lambda i: (i, 0)),
                  pl.BlockSpec(memory_space=pl.ANY)],
        out_specs=pl.BlockSpec(memory_space=pl.ANY),
        out_shape=jax.ShapeDtypeStruct((n_rows, w), U32),
        scratch_shapes=[pltpu.SemaphoreType.DMA(())],
        input_output_aliases={2: 0},
        compiler_params=_params("arbitrary"),
        name="moe_dispatch",
    )(dest, xp, xs0)


def _experts_kernel(be_ref, nb_ref, xs_ref, wg_ref, wu_ref, wd_ref, y_ref, wg_b, wu_b, wd_b):
    i = pl.program_id(0)
    fresh = jnp.logical_or(i == 0, be_ref[i] != be_ref[jnp.maximum(i - 1, 0)])

    @pl.when(fresh)
    def _():
        wg_b[...] = wg_ref[...].astype(BF16)
        wu_b[...] = wu_ref[...].astype(BF16)
        wd_b[...] = wd_ref[...].astype(BF16)

    @pl.when(i < nb_ref[0])
    def _():
        hi, lo = _unpack_bf16_pairs(xs_ref[...])
        half = hi.shape[1]
        g = _dot(hi, wg_b[:half, :]) + _dot(lo, wg_b[half:, :])
        u = _dot(hi, wu_b[:half, :]) + _dot(lo, wu_b[half:, :])
        y_ref[...] = _dot((jax.nn.silu(g) * u).astype(BF16), wd_b[...])

    @pl.when(i >= nb_ref[0])
    def _():
        y_ref[...] = jnp.zeros(y_ref.shape, F32)


def _experts(xs, block_e, n_blocks, wg, wu, wd, layer):
    n_rows, w = xs.shape
    bm = MOE_ROWS
    d, ff = wg.shape[2], wg.shape[3]
    grid_spec = pltpu.PrefetchScalarGridSpec(
        num_scalar_prefetch=2,
        grid=(n_rows // bm,),
        in_specs=[
            pl.BlockSpec((bm, w), lambda i, be, nb: (jnp.minimum(i, nb[0] - 1), 0)),
            pl.BlockSpec((None, None, d, ff), lambda i, be, nb: (layer, be[i], 0, 0)),
            pl.BlockSpec((None, None, d, ff), lambda i, be, nb: (layer, be[i], 0, 0)),
            pl.BlockSpec((None, None, ff, d), lambda i, be, nb: (layer, be[i], 0, 0)),
        ],
        out_specs=pl.BlockSpec((bm, d), lambda i, be, nb: (i, 0)),
        scratch_shapes=[pltpu.VMEM((d, ff), BF16), pltpu.VMEM((d, ff), BF16), pltpu.VMEM((ff, d), BF16)],
    )
    return pl.pallas_call(
        _experts_kernel,
        grid_spec=grid_spec,
        out_shape=jax.ShapeDtypeStruct((n_rows, d), F32),
        compiler_params=_params("arbitrary"),
        name="moe_experts",
    )(block_e, n_blocks, xs, wg, wu, wd)


def _combine_kernel(dest_ref, x1_ref, gate_ref, sg_ref, su_ref, sd_ref, lg_ref, lb_ref, yb_hbm, x2_ref, buf, sem,
                    *, alpha):
    tokens = x1_ref.shape[0]

    def copy(tok, k):
        d = dest_ref[k, tok]
        return pltpu.make_async_copy(yb_hbm.at[pl.ds(d, 1), :], buf.at[k, pl.ds(tok, 1), :], sem)

    def start(tok, carry):
        for k in range(TOP_K):
            copy(tok, k).start()
        return carry

    def wait(tok, carry):
        for k in range(TOP_K):
            copy(tok, k).wait()
        return carry

    lax.fori_loop(0, tokens, start, 0)
    x1 = x1_ref[...]
    xb = x1.astype(BF16)
    h = (jax.nn.silu(_dot(xb, sg_ref[...])) * _dot(xb, su_ref[...])).astype(BF16)
    acc = alpha * x1 + _dot(h, sd_ref[...])
    gates = jnp.concatenate([gate_ref[...], jnp.zeros((tokens - TOP_K, tokens), F32)], axis=0).T
    lax.fori_loop(0, tokens, wait, 0)
    for k in range(TOP_K):
        acc = acc + gates[:, k:k + 1] * buf[k]
    x2_ref[...] = _layer_norm(acc, lg_ref[...], lb_ref[...])


def _combine(x1, gates, dest, yb, lw, alpha):
    t, d = x1.shape
    tm = MOE_TOKENS
    full = lambda i: (0, 0)
    return pl.pallas_call(
        functools.partial(_combine_kernel, alpha=alpha),
        grid=(t // tm,),
        in_specs=[
            pl.BlockSpec((TOP_K, tm), lambda i: (0, i), memory_space=pltpu.SMEM),
            pl.BlockSpec((tm, d), lambda i: (i, 0)),
            pl.BlockSpec((TOP_K, tm), lambda i: (0, i)),
            pl.BlockSpec(lw["sg"].shape, full), pl.BlockSpec(lw["su"].shape, full), pl.BlockSpec(lw["sd"].shape, full),
            pl.BlockSpec(lw["ln2g"].shape, full), pl.BlockSpec(lw["ln2b"].shape, full),
            pl.BlockSpec(memory_space=pl.ANY),
        ],
        out_specs=pl.BlockSpec((tm, d), lambda i: (i, 0)),
        out_shape=jax.ShapeDtypeStruct((t, d), F32),
        scratch_shapes=[pltpu.VMEM((TOP_K, tm, d), F32), pltpu.SemaphoreType.DMA(())],
        compiler_params=_params("arbitrary"),
        name="moe_combine",
    )(dest, x1, gates, lw["sg"], lw["su"], lw["sd"], lw["ln2g"], lw["ln2b"], yb)


def _swap_halves(w):
    half = w.shape[-1] // 2
    return jnp.concatenate([w[..., half:], w[..., :half]], axis=-1)


def _s5_discretise(log_dt, a_re, a_im, b_re, b_im, c_re, c_im):
    n_g, n_p = a_re.shape
    n_h = b_re.shape[2]
    dt = jnp.exp(log_dt)[:, None]
    mag = jnp.exp(a_re * dt)
    ab_re, ab_im = mag * jnp.cos(a_im * dt), mag * jnp.sin(a_im * dt)
    den = jnp.square(a_re) + jnp.square(a_im)
    f_re = ((ab_re - 1.0) * a_re + ab_im * a_im) / den
    f_im = (ab_im * a_re - (ab_re - 1.0) * a_im) / den
    bb_re = f_re[..., None] * b_re - f_im[..., None] * b_im
    bb_im = f_re[..., None] * b_im + f_im[..., None] * b_re
    eye = jnp.eye(n_g, dtype=F32)
    blk = lambda m: jnp.einsum("gph,gk->ghkp", m, eye).reshape(n_g * n_h, n_g * n_p)
    wbu = jnp.concatenate([blk(bb_re), blk(bb_im)], axis=1)
    blk_c = lambda m: jnp.einsum("ghp,gk->kpgh", m, eye).reshape(n_g * n_p, n_g * n_h)
    wc = jnp.concatenate([blk_c(c_re), blk_c(-c_im)], axis=0)
    a = jnp.stack([ab_re.reshape(-1), ab_im.reshape(-1)], axis=0)
    return a, wbu.astype(BF16), wc.astype(BF16)


def _layer_weights(l, w, s_new):
    (w_in, q_norm_g, w_uq, kv_norm_g, w_uk, w_uv, ssm_log_dt, ssm_a_re, ssm_a_im, ssm_b_re, ssm_b_im, ssm_c_re,
     ssm_c_im, ssm_d, ssm_w_glu, ssm_b_glu, gmlp_ln_g, gmlp_ln_b, gmlp_w_s, gmlp_b_s, w_o, ln1_g, ln1_b, router_w,
     router_bias, sh_w_gate, sh_w_up, sh_w_down, ln2_g, ln2_b) = [a[l] for a in w]
    kv_rank, heads, nope = w_uk.shape
    q_rank = w_uq.shape[0]
    rope = w_uq.shape[1] // heads - nope
    v_dim = w_uv.shape[2]
    ssm_w = ssm_d.shape[0]
    g_heads, g_dim = gmlp_ln_g.shape
    gw = g_heads * g_dim
    chunk = gmlp_w_s.shape[1]
    o0, o1, o2, o3 = q_rank, q_rank + kv_rank, q_rank + kv_rank + rope, q_rank + kv_rank + rope + ssm_w
    w_kr = w_in[:, o1:o2]
    uq = w_uq.reshape(q_rank, heads, nope + rope)
    uq_r = uq[:, :, nope:]
    a, wbu, wc = _s5_discretise(ssm_log_dt, ssm_a_re, ssm_a_im, ssm_b_re, ssm_b_im, ssm_c_re, ssm_c_im)
    tril = jnp.tril(jnp.ones((chunk, chunk), bool))
    mix_p = jnp.where(tril, gmlp_w_s, 0.0)
    tril_s = jnp.tril(jnp.ones((s_new, s_new), bool))
    small = jnp.where(tril_s, gmlp_w_s[:, :s_new, :s_new], 0.0)
    reps = chunk // s_new
    mix_s = jnp.einsum("ab,hts->hatbs", jnp.eye(reps, dtype=F32), small).reshape(g_heads, chunk, chunk)
    bias_p = jnp.broadcast_to(gmlp_b_s.T[:, :, None], (chunk, g_heads, g_dim)).reshape(chunk, gw)
    bias_s = jnp.tile(jnp.broadcast_to(gmlp_b_s[:, :s_new].T[:, :, None], (s_new, g_heads, g_dim)).reshape(s_new, gw),
                      (reps, 1))
    attn_w = heads * v_dim
    n_e = router_w.shape[1]
    return {
        "wq": w_in[:, :o0].astype(BF16), "wkv": w_in[:, o0:o1].astype(BF16),
        "wkr": jnp.concatenate([w_kr, _swap_halves(w_kr)], axis=1).astype(BF16),
        "wu": w_in[:, o2:o3].astype(BF16), "wz": w_in[:, o3:].astype(BF16),
        "gq": q_norm_g[None, :], "gkv": kv_norm_g[None, :],
        "lng": gmlp_ln_g.reshape(1, gw), "lnb": gmlp_ln_b.reshape(1, gw),
        "mix": jnp.stack([mix_p, mix_s]).astype(BF16), "mixb": jnp.stack([bias_p, bias_s]),
        "wqn": jnp.transpose(uq[:, :, :nope], (1, 0, 2)).astype(BF16),
        "wqr": jnp.transpose(jnp.concatenate([uq_r, _swap_halves(uq_r)], axis=2), (1, 0, 2)).astype(BF16),
        "wuk": jnp.transpose(w_uk, (1, 2, 0)).astype(BF16),
        "wuv": jnp.transpose(w_uv, (1, 0, 2)).astype(BF16),
        "a": a, "wbu": wbu, "wc": wc, "d": ssm_d[None, :], "wglu": ssm_w_glu.astype(BF16), "bglu": ssm_b_glu[None, :],
        "woa": w_o[:attn_w].astype(BF16), "wos": w_o[attn_w:attn_w + ssm_w].astype(BF16),
        "wog": w_o[attn_w + ssm_w:].astype(BF16),
        "ln1g": ln1_g[None, :], "ln1b": ln1_b[None, :],
        "rwt": router_w.T, "rbias": jnp.broadcast_to(router_bias[:, None], (n_e, ROW_TILE)),
        "sg": sh_w_gate.astype(BF16), "su": sh_w_up.astype(BF16), "sd": sh_w_down.astype(BF16),
        "ln2g": ln2_g[None, :], "ln2b": ln2_b[None, :],
    }


def _rope_tables(pos, rope):
    half = rope // 2
    inv_freq = ROPE_THETA ** (-jnp.arange(half, dtype=F32) / half)
    ang = pos.astype(F32)[:, None] * inv_freq[None, :]
    cos, sin = jnp.cos(ang), jnp.sin(ang)
    return {"cos": jnp.concatenate([cos, cos], axis=1), "sin": jnp.concatenate([-sin, sin], axis=1)}


def _block_table(counts, n_rows):
    bm = MOE_ROWS
    padded = (counts + bm - 1) // bm * bm
    ends = jnp.cumsum(padded)
    n_blocks = ends[-1] // bm
    blocks = jnp.arange(n_rows // bm, dtype=I32)
    be = jnp.minimum(jnp.searchsorted(ends, blocks * bm, side="right"), counts.shape[0] - 1).astype(I32)
    be = jnp.where(blocks < n_blocks, be, be[jnp.maximum(n_blocks - 1, 0)])
    return ends - padded, be, n_blocks.reshape(1).astype(I32)


def kernel(x_prompt, x_sample, cache_kv, cache_kr, state_ssm_re, state_ssm_im, page_table, w_in, q_norm_g, w_uq, kv_norm_g, w_uk, w_uv, ssm_log_dt, ssm_a_re, ssm_a_im, ssm_b_re, ssm_b_im, ssm_c_re, ssm_c_im, ssm_d, ssm_w_glu, ssm_b_glu, gmlp_ln_g, gmlp_ln_b, gmlp_w_s, gmlp_b_s, w_o, ln1_g, ln1_b, router_w, router_bias, exp_w_gate, exp_w_up, exp_w_down, sh_w_gate, sh_w_up, sh_w_down, ln2_g, ln2_b):
    depth = w_in.shape[0]
    batch, seq, d = x_prompt.shape
    dec_b, s_new, _ = x_sample.shape
    n_pages, page = page_table.shape[1], cache_kv.shape[2]
    past = n_pages * page
    tp, ts = batch * seq, dec_b * s_new
    t = tp + ts
    kv_rank, heads, nope = w_uk.shape[1:]
    rope = cache_kr.shape[3]
    n_g, n_p = ssm_a_re.shape[1:]
    n_e = router_w.shape[2]
    assert tp % ROW_TILE == 0 and ts % ROW_TILE == 0 and seq % ROW_TILE == 0 and ROW_TILE % s_new == 0
    assert seq % KV_TILE == 0 and KV_TILE % Q_TILE == 0 and n_pages % PAGE_GROUP == 0 and t % MOE_TOKENS == 0
    assert kv_rank + rope <= QK_PAD and s_new <= LANES_V7X
    alpha = (2.0 * depth) ** 0.25
    scale = (nope + rope) ** -0.5
    n_rows = t * TOP_K + n_e * MOE_ROWS

    small = (w_in, q_norm_g, w_uq, kv_norm_g, w_uk, w_uv, ssm_log_dt, ssm_a_re, ssm_a_im, ssm_b_re, ssm_b_im,
             ssm_c_re, ssm_c_im, ssm_d, ssm_w_glu, ssm_b_glu, gmlp_ln_g, gmlp_ln_b, gmlp_w_s, gmlp_b_s, w_o, ln1_g,
             ln1_b, router_w, router_bias, sh_w_gate, sh_w_up, sh_w_down, ln2_g, ln2_b)
    pos = jnp.concatenate([jnp.tile(jnp.arange(seq, dtype=I32), batch),
                           jnp.tile(past + jnp.arange(s_new, dtype=I32), dec_b)])
    tabs = _rope_tables(pos, rope)
    x = jnp.concatenate([x_prompt.reshape(tp, d), x_sample.reshape(ts, d)], axis=0)
    h0_p = jnp.zeros((batch, 1, 2 * n_g * n_p), F32)
    per_layer = []
    for l in range(depth):
        lw = _layer_weights(l, small, s_new)
        cq, c, kr, kcat, u, gm, v = _in_proj(x, lw, tabs, tp // ROW_TILE)
        q_p = _mla_q(cq, lw, tabs, 0, tp, BF16)
        q_s = _mla_q(cq, lw, tabs, tp, ts, F32)
        attn_p = _attn_prompt(q_p, kcat, lw["wuv"], batch, seq, scale)
        attn_s = _attn_sample(q_s, c, kr, tp, lw["wuv"], cache_kv, cache_kr, page_table, l, scale)
        h0_s = jnp.concatenate([state_ssm_re[l].reshape(dec_b, 1, -1), state_ssm_im[l].reshape(dec_b, 1, -1)], axis=2)
        ssm_p, hT_p = _s5(u, h0_p, lw, 0, tp, seq)
        ssm_s, hT_s = _s5(u, h0_s, lw, tp, ts, s_new)
        attn = jnp.concatenate([attn_p, attn_s], axis=0)
        ssm = jnp.concatenate([ssm_p, ssm_s], axis=0)
        x1, xp = _out_proj(x, attn, ssm, gm, lw, alpha)
        idx, gates, rank, cnt = _router(x1, lw)
        offs, block_e, n_blocks = _block_table(cnt[:, 0].astype(I32), n_rows)
        dest = offs[idx] + rank
        xs = _dispatch(xp, dest, n_rows)
        yb = _experts(xs, block_e, n_blocks, exp_w_gate, exp_w_up, exp_w_down, l)
        x = _combine(x1, gates, dest, yb, lw, alpha)
        per_layer.append((c, kr, hT_p, hT_s, v))

    half = n_g * n_p
    st = lambda f: jnp.stack([f(p) for p in per_layer], axis=0)
    return (
        x[:tp].reshape(batch, seq, d),
        x[tp:].reshape(dec_b, s_new, d),
        st(lambda p: p[0][:tp].reshape(batch, seq, kv_rank)),
        st(lambda p: p[1][:tp].reshape(batch, seq, rope)),
        st(lambda p: p[2][:, 0, :half].reshape(batch, n_g, n_p)),
        st(lambda p: p[2][:, 0, half:].reshape(batch, n_g, n_p)),
        st(lambda p: p[0][tp:].reshape(dec_b, s_new, kv_rank)),
        st(lambda p: p[1][tp:].reshape(dec_b, s_new, rope)),
        st(lambda p: p[3][:, 0, :half].reshape(dec_b, n_g, n_p)),
        st(lambda p: p[3][:, 0, half:].reshape(dec_b, n_g, n_p)),
        st(lambda p: p[4][tp:].reshape(dec_b, s_new, -1)),
    )
```

```python
import functools
import math

import jax
import jax.numpy as jnp
from jax import lax
from jax.experimental import pallas as pl
from jax.experimental.pallas import tpu as pltpu

F32 = jnp.float32
BF16 = jnp.bfloat16
I32 = jnp.int32
U32 = jnp.uint32

LANES_V7X = 128
VMEM_LIMIT_V7X = 56 * 1024 * 1024

ROPE_THETA = 10000.0
EPS = 1e-6
TOP_K = 8
N_EXPERT_GROUPS = 8
TOPK_GROUPS = 4
ROUTE_SCALE = 2.5

ROW_TILE = 256
MOE_ROWS = 256
MOE_TOKENS = 128
Q_TILE = 128
KV_TILE = 256
Q_CHUNK_HEADS = 2
PAGE_GROUP = 16
KEY_CHUNK = 1024
QK_PAD = 384


def _params(*sem):
    return pltpu.CompilerParams(dimension_semantics=sem, vmem_limit_bytes=VMEM_LIMIT_V7X)


def _gelu(x):
    c = math.sqrt(2.0 / math.pi)
    return x * (0.5 * (1.0 + jnp.tanh(c * (x + 0.044715 * (x * x * x)))))


def _rms(x, g):
    return x * lax.rsqrt(jnp.mean(x * x, axis=-1, keepdims=True) + EPS) * g


def _layer_norm(x, g, b):
    mu = jnp.mean(x, axis=-1, keepdims=True)
    xc = x - mu
    var = jnp.mean(xc * xc, axis=-1, keepdims=True)
    return xc * lax.rsqrt(var + EPS) * g + b


def _dot(a, b):
    return jnp.dot(a, b, preferred_element_type=F32)


def _dot_nt(a, b):
    return lax.dot_general(a, b, (((1,), (1,)), ((), ())), preferred_element_type=F32)


def _in_proj_kernel(x_ref, wq_ref, wkv_ref, wkr_ref, wu_ref, wz_ref, gq_ref, gkv_ref, cos_ref, sin_ref,
                    lng_ref, lnb_ref, mix_ref, mixb_ref,
                    cq_ref, c_ref, kr_ref, kcat_ref, u_ref, gm_ref, v_ref, *, kv_rank, rope, g_heads, g_dim, chunk):
    xb = x_ref[...].astype(BF16)
    cq_ref[...] = _rms(_dot(xb, wq_ref[...]), gq_ref[...]).astype(BF16)
    c = _rms(_dot(xb, wkv_ref[...]), gkv_ref[...])
    c_ref[...] = c
    kr2 = _dot(xb, wkr_ref[...])
    kr = kr2[:, :rope] * cos_ref[...] + kr2[:, rope:] * sin_ref[...]
    kr_ref[...] = kr
    kcat_ref[:, :kv_rank] = c.astype(BF16)
    kcat_ref[:, kv_rank:kv_rank + rope] = kr.astype(BF16)
    kcat_ref[:, kv_rank + rope:] = jnp.zeros((kcat_ref.shape[0], kcat_ref.shape[1] - kv_rank - rope), BF16)
    u_ref[...] = _dot(xb, wu_ref[...])
    z = _gelu(_dot(xb, wz_ref[...]))
    width = g_heads * g_dim
    for h in range(g_heads):
        sl = slice(h * g_dim, (h + 1) * g_dim)
        vh = _layer_norm(z[:, width + h * g_dim: width + (h + 1) * g_dim], lng_ref[:, sl], lnb_ref[:, sl])
        v_ref[:, sl] = vh
        vb = vh.astype(BF16)
        for j in range(x_ref.shape[0] // chunk):
            rows = slice(j * chunk, (j + 1) * chunk)
            sv = _dot(mix_ref[h], vb[rows]) + mixb_ref[:, sl]
            gm_ref[rows, sl] = (z[rows, sl] * sv).astype(BF16)


def _in_proj(x, lw, tabs, n_prompt_tiles):
    t, d = x.shape
    tm = ROW_TILE
    kv_rank, rope = lw["wkv"].shape[1], lw["wkr"].shape[1] // 2
    gw = lw["lng"].shape[1]
    g_heads, chunk = lw["mix"].shape[1], lw["mix"].shape[2]
    row = lambda i: (i, 0)
    full = lambda i: (0, 0)
    trunk = lambda i: (jnp.where(i >= n_prompt_tiles, 1, 0), 0, 0, 0)
    trunk3 = lambda i: (jnp.where(i >= n_prompt_tiles, 1, 0), 0, 0)
    kern = functools.partial(_in_proj_kernel, kv_rank=kv_rank, rope=rope, g_heads=g_heads, g_dim=gw // g_heads,
                             chunk=chunk)
    q_rank, ssm_w = lw["wq"].shape[1], lw["wu"].shape[1]
    return pl.pallas_call(
        kern,
        grid=(t // tm,),
        in_specs=[
            pl.BlockSpec((tm, d), row),
            pl.BlockSpec(lw["wq"].shape, full), pl.BlockSpec(lw["wkv"].shape, full),
            pl.BlockSpec(lw["wkr"].shape, full), pl.BlockSpec(lw["wu"].shape, full),
            pl.BlockSpec(lw["wz"].shape, full),
            pl.BlockSpec(lw["gq"].shape, full), pl.BlockSpec(lw["gkv"].shape, full),
            pl.BlockSpec((tm, rope), row), pl.BlockSpec((tm, rope), row),
            pl.BlockSpec(lw["lng"].shape, full), pl.BlockSpec(lw["lnb"].shape, full),
            pl.BlockSpec((None,) + lw["mix"].shape[1:], trunk),
            pl.BlockSpec((None,) + lw["mixb"].shape[1:], trunk3),
        ],
        out_specs=[
            pl.BlockSpec((tm, q_rank), row), pl.BlockSpec((tm, kv_rank), row), pl.BlockSpec((tm, rope), row),
            pl.BlockSpec((tm, QK_PAD), row), pl.BlockSpec((tm, ssm_w), row), pl.BlockSpec((tm, gw), row),
            pl.BlockSpec((tm, gw), row),
        ],
        out_shape=[
            jax.ShapeDtypeStruct((t, q_rank), BF16), jax.ShapeDtypeStruct((t, kv_rank), F32),
            jax.ShapeDtypeStruct((t, rope), F32), jax.ShapeDtypeStruct((t, QK_PAD), BF16),
            jax.ShapeDtypeStruct((t, ssm_w), F32), jax.ShapeDtypeStruct((t, gw), BF16),
            jax.ShapeDtypeStruct((t, gw), F32),
        ],
        compiler_params=_params("parallel"),
        name="in_proj",
    )(x, lw["wq"], lw["wkv"], lw["wkr"], lw["wu"], lw["wz"], lw["gq"], lw["gkv"], tabs["cos"], tabs["sin"],
      lw["lng"], lw["lnb"], lw["mix"], lw["mixb"])


def _mla_q_kernel(cq_ref, wqn_ref, wqr_ref, wuk_ref, cos_ref, sin_ref, q_ref, *, kv_rank, rope, qscale):
    cq = cq_ref[...]
    for h in range(q_ref.shape[0]):
        qn = _dot(cq, wqn_ref[h]).astype(BF16)
        q_ref[h, :, :kv_rank] = (_dot(qn, wuk_ref[h]) * qscale).astype(q_ref.dtype)
        qr2 = _dot(cq, wqr_ref[h])
        qr = qr2[:, :rope] * cos_ref[...] + qr2[:, rope:] * sin_ref[...]
        q_ref[h, :, kv_rank:kv_rank + rope] = (qr * qscale).astype(q_ref.dtype)
        q_ref[h, :, kv_rank + rope:] = jnp.zeros((q_ref.shape[1], q_ref.shape[2] - kv_rank - rope), q_ref.dtype)


def _mla_q(cq, lw, tabs, row0, n_rows, out_dtype, qscale):
    tm = ROW_TILE
    heads, q_rank, _ = lw["wqn"].shape
    kv_rank, rope = lw["wuk"].shape[2], lw["wqr"].shape[2] // 2
    off = row0 // tm
    row = lambda i: (i + off, 0)
    full = lambda i: (0, 0, 0)
    return pl.pallas_call(
        functools.partial(_mla_q_kernel, kv_rank=kv_rank, rope=rope, qscale=qscale),
        grid=(n_rows // tm,),
        in_specs=[
            pl.BlockSpec((tm, q_rank), row),
            pl.BlockSpec(lw["wqn"].shape, full), pl.BlockSpec(lw["wqr"].shape, full),
            pl.BlockSpec(lw["wuk"].shape, full),
            pl.BlockSpec((tm, rope), row), pl.BlockSpec((tm, rope), row),
        ],
        out_specs=pl.BlockSpec((heads, tm, QK_PAD), lambda i: (0, i, 0)),
        out_shape=jax.ShapeDtypeStruct((heads, n_rows, QK_PAD), out_dtype),
        compiler_params=_params("parallel"),
        name="mla_q",
    )(cq, lw["wqn"], lw["wqr"], lw["wuk"], tabs["cos"], tabs["sin"])


def _lane_tiles(x, op):
    tiles = [x[:, i * LANES_V7X:(i + 1) * LANES_V7X] for i in range(x.shape[1] // LANES_V7X)]
    return functools.reduce(op, tiles)


def _widen(x, width):
    return jnp.tile(x, (1, width // LANES_V7X))


def _attn_prompt_kernel(q_ref, k_ref, wuv_ref, o_ref, s_scr, m_scr, l_scr, acc_scr, *, tq, tk, kv_rank, hpc):
    qi = pl.program_id(1)
    heads = q_ref.shape[0]
    cr = hpc * tq
    q = q_ref[...].reshape(heads * tq, q_ref.shape[2])
    m_scr[...] = jnp.full(m_scr.shape, -jnp.inf, F32)
    l_scr[...] = jnp.zeros(l_scr.shape, F32)
    acc_scr[...] = jnp.zeros(acc_scr.shape, F32)
    n_full = (qi * tq) // tk

    def keys(j):
        return k_ref[pl.ds(pl.multiple_of(j * tk, tk), tk), :]

    def scores(j, masked):
        s = _dot_nt(q, keys(j))
        if masked:
            qpos = qi * tq + lax.broadcasted_iota(I32, s.shape, 0) % tq
            kpos = j * tk + lax.broadcasted_iota(I32, s.shape, 1)
            s = jnp.where(kpos <= qpos, s, -jnp.inf)
        s_scr[j % 2] = s

    def weigh(j):
        v = keys(j)[:, :kv_rank]
        for c in range(heads // hpc):
            rows = slice(c * cr, (c + 1) * cr)
            s = s_scr[j % 2, rows]
            m_prev = m_scr[rows]
            m_new = jnp.maximum(m_prev, jnp.max(s, axis=-1, keepdims=True))
            alpha = jnp.exp2(m_prev - m_new)
            p = jnp.exp2(s - _widen(m_new, tk))
            l_scr[rows] = alpha * l_scr[rows] + _lane_tiles(p, jnp.add)
            acc_scr[rows] = _widen(alpha, kv_rank) * acc_scr[rows] + _dot(p.astype(BF16), v)
            m_scr[rows] = m_new

    @pl.when(n_full == 0)
    def _():
        scores(0, True)

    @pl.when(n_full > 0)
    def _():
        scores(0, False)

        def step(j, carry):
            weigh(j)
            scores(j + 1, False)
            return carry

        lax.fori_loop(0, n_full - 1, step, 0)
        weigh(n_full - 1)
        scores(n_full, True)

    weigh(n_full)
    o = (acc_scr[...] / jnp.sum(l_scr[...], axis=-1, keepdims=True)).astype(BF16)
    v_dim = wuv_ref.shape[2]
    for h in range(heads):
        o_ref[:, h * v_dim:(h + 1) * v_dim] = _dot(o[h * tq:(h + 1) * tq], wuv_ref[h])


def _attn_prompt(q, kcat, wuv, batch, seq):
    heads, tp, _ = q.shape
    kv_rank, v_dim = wuv.shape[1], wuv.shape[2]
    tq, tk = Q_TILE, KV_TILE
    nq = seq // tq
    rows = heads * tq
    return pl.pallas_call(
        functools.partial(_attn_prompt_kernel, tq=tq, tk=tk, kv_rank=kv_rank, hpc=Q_CHUNK_HEADS),
        grid=(batch, nq),
        in_specs=[
            pl.BlockSpec((heads, tq, QK_PAD), lambda b, i: (0, b * nq + i, 0)),
            pl.BlockSpec((seq, QK_PAD), lambda b, i: (b, 0)),
            pl.BlockSpec(wuv.shape, lambda b, i: (0, 0, 0)),
        ],
        out_specs=pl.BlockSpec((tq, heads * v_dim), lambda b, i: (b * nq + i, 0)),
        out_shape=jax.ShapeDtypeStruct((tp, heads * v_dim), F32),
        scratch_shapes=[pltpu.VMEM((2, rows, tk), F32), pltpu.VMEM((rows, LANES_V7X), F32),
                        pltpu.VMEM((rows, LANES_V7X), F32), pltpu.VMEM((rows, kv_rank), F32)],
        compiler_params=_params("parallel", "arbitrary"),
        name="attn_prompt",
    )(q, kcat, wuv)


def _attn_sample_kernel(pt_ref, q_ref, cnew_ref, krnew_ref, wuv_ref, ckv_hbm, ckrt_hbm, o_ref,
                        kvbuf, krbuf, kb_all, krt_scr, knew_scr, s_scr, sems,
                        *, layer, n_pages, group, page, kv_rank, rope, chunk):
    b = pl.program_id(0)
    heads, s_new, _ = q_ref.shape
    rows = heads * s_new
    n_groups = n_pages // group
    ppc = chunk // page
    cpg = group // ppc
    q = q_ref[...].reshape(rows, q_ref.shape[2]).astype(BF16)
    q_lat, q_rope = q[:, :kv_rank], q[:, kv_rank:]

    def copies(bb, g, slot, p):
        pg = pt_ref[bb * n_pages + g * group + p]
        return (pltpu.make_async_copy(ckv_hbm.at[layer, pg], kvbuf.at[slot, p], sems.at[0, slot]),
                pltpu.make_async_copy(ckrt_hbm.at[layer, pg], krbuf.at[slot, p], sems.at[1, slot]))

    def start(bb, g, slot):
        for p in range(group):
            for cp in copies(bb, g, slot, p):
                cp.start()

    def wait(slot):
        for p in range(group):
            for cp in copies(0, 0, slot, p):
                cp.wait()

    @pl.when(b == 0)
    def _():
        start(0, 0, 0)

    krt_scr[:, rope:, :] = jnp.zeros((cpg, krt_scr.shape[1] - rope, chunk), BF16)

    def stream(g, m_part):
        slot = g % 2

        @pl.when(g + 1 < n_groups)
        def _():
            start(b, g + 1, 1 - slot)

        wait(slot)
        for c in range(cpg):
            kvb = kvbuf[slot, c * ppc:(c + 1) * ppc].reshape(chunk, kv_rank).astype(BF16)
            kb_all[g * cpg + c] = kvb
            for p in range(ppc):
                krt_scr[c, :rope, p * page:(p + 1) * page] = krbuf[slot, c * ppc + p].astype(BF16)
            s = _dot_nt(q_lat, kvb) + _dot(q_rope, krt_scr[c])
            s_scr[g * cpg + c] = s
            m_part = jnp.maximum(m_part, _lane_tiles(s, jnp.maximum))
        return m_part

    m_part = lax.fori_loop(0, n_groups, stream, jnp.full((rows, LANES_V7X), -jnp.inf, F32))

    @pl.when(b + 1 < pl.num_programs(0))
    def _():
        start(b + 1, 0, 0)

    knew_scr[...] = jnp.zeros(knew_scr.shape, BF16)
    knew_scr[:s_new, :kv_rank] = cnew_ref[...].astype(BF16)
    knew_scr[:s_new, kv_rank:kv_rank + rope] = krnew_ref[...].astype(BF16)
    kn = knew_scr[...]
    sn = _dot_nt(q, kn)
    qpos = lax.broadcasted_iota(I32, sn.shape, 0) % s_new
    kpos = lax.broadcasted_iota(I32, sn.shape, 1)
    sn = jnp.where(kpos <= qpos, sn, -jnp.inf)
    m = jnp.max(jnp.maximum(m_part, sn), axis=-1, keepdims=True)
    m_wide = jnp.broadcast_to(m, (rows, chunk))

    def weigh(i, carry):
        l_part, acc = carry
        p = jnp.exp2(s_scr[i] - m_wide)
        return l_part + _lane_tiles(p, jnp.add), acc + _dot(p.astype(BF16), kb_all[i])

    pn = jnp.exp2(sn - m_wide[:, :LANES_V7X])
    l_part, acc = lax.fori_loop(0, n_groups * cpg, weigh, (pn, _dot(pn.astype(BF16), kn[:, :kv_rank])), unroll=2)
    o = (acc / jnp.sum(l_part, axis=-1, keepdims=True)).astype(BF16)
    v_dim = wuv_ref.shape[2]
    for h in range(heads):
        o_ref[:, h * v_dim:(h + 1) * v_dim] = _dot(o, wuv_ref[h])[h * s_new:(h + 1) * s_new]


def _attn_sample(q, c, kr, row0, wuv, cache_kv, cache_krt, page_table, layer):
    heads, ts, _ = q.shape
    dec_b, n_pages = page_table.shape
    s_new = ts // dec_b
    page, kv_rank = cache_kv.shape[2], cache_kv.shape[3]
    rope = cache_krt.shape[2]
    v_dim = wuv.shape[2]
    group, chunk = PAGE_GROUP, KEY_CHUNK
    n_chunks = n_pages * page // chunk
    rows = heads * s_new
    off = row0 // s_new
    grid_spec = pltpu.PrefetchScalarGridSpec(
        num_scalar_prefetch=1,
        grid=(dec_b,),
        in_specs=[
            pl.BlockSpec((heads, s_new, QK_PAD), lambda b, pt: (0, b, 0)),
            pl.BlockSpec((s_new, kv_rank), lambda b, pt: (b + off, 0)),
            pl.BlockSpec((s_new, rope), lambda b, pt: (b + off, 0)),
            pl.BlockSpec(wuv.shape, lambda b, pt: (0, 0, 0)),
            pl.BlockSpec(memory_space=pl.ANY),
            pl.BlockSpec(memory_space=pl.ANY),
        ],
        out_specs=pl.BlockSpec((s_new, heads * v_dim), lambda b, pt: (b, 0)),
        scratch_shapes=[
            pltpu.VMEM((2, group, page, kv_rank), F32),
            pltpu.VMEM((2, group, rope, page), F32),
            pltpu.VMEM((n_chunks, chunk, kv_rank), BF16),
            pltpu.VMEM((group * page // chunk, QK_PAD - kv_rank, chunk), BF16),
            pltpu.VMEM((LANES_V7X, QK_PAD), BF16),
            pltpu.VMEM((n_chunks, rows, chunk), F32),
            pltpu.SemaphoreType.DMA((2, 2)),
        ],
    )
    return pl.pallas_call(
        functools.partial(_attn_sample_kernel, layer=layer, n_pages=n_pages, group=group, page=page,
                          kv_rank=kv_rank, rope=rope, chunk=chunk),
        grid_spec=grid_spec,
        out_shape=jax.ShapeDtypeStruct((ts, heads * v_dim), F32),
        compiler_params=_params("arbitrary"),
        name="attn_sample",
    )(page_table.reshape(-1), q, c, kr, wuv, cache_kv, cache_krt)


def _s5_kernel(u_ref, h0_ref, wbu_ref, a_ref, wc_ref, d_ref, wglu_ref, bglu_ref, y_ref, ht_ref, hs_scr, carry_scr,
               *, seq_rows, seqs_per_tile, tiles_per_seq):
    i = pl.program_id(0)
    u = u_ref[...]
    hs_scr[...] = _dot(u.astype(BF16), wbu_ref[...])
    n = a_ref.shape[1]
    a_re, a_im = a_ref[0:1, :], a_ref[1:2, :]

    if tiles_per_seq > 1:
        @pl.when(i % tiles_per_seq == 0)
        def _():
            carry_scr[...] = h0_ref[0]

    def one_sequence(s, carry):
        h = carry_scr[...] if tiles_per_seq > 1 else h0_ref[s]

        def row(t, hc):
            hr, hi = hc
            r = s * seq_rows + t
            nr = a_re * hr - a_im * hi + hs_scr[pl.ds(r, 1), :n]
            ni = a_re * hi + a_im * hr + hs_scr[pl.ds(r, 1), n:]
            hs_scr[pl.ds(r, 1), :n] = nr
            hs_scr[pl.ds(r, 1), n:] = ni
            return nr, ni

        hr, hi = lax.fori_loop(0, seq_rows, row, (h[:, :n], h[:, n:]), unroll=8)
        last = jnp.concatenate([hr, hi], axis=1)
        ht_ref[s] = last
        if tiles_per_seq > 1:
            carry_scr[...] = last
        return carry

    lax.fori_loop(0, seqs_per_tile, one_sequence, 0)
    y = _dot(hs_scr[...].astype(BF16), wc_ref[...]) + d_ref[...] * u
    g = _gelu(y)
    y_ref[...] = g * jax.nn.sigmoid(_dot(g.astype(BF16), wglu_ref[...]) + bglu_ref[...])


def _s5(u, h0, lw, row0, n_rows, seq_len):
    tc = ROW_TILE
    width = u.shape[1]
    n2 = lw["wbu"].shape[1]
    seq_rows = min(seq_len, tc)
    seqs_per_tile = tc // seq_rows
    tiles_per_seq = max(1, seq_len // tc)
    n_seq = n_rows // seq_len
    off = row0 // tc
    full = lambda i: (0, 0)
    seq_idx = (lambda i: (i // tiles_per_seq, 0, 0)) if tiles_per_seq > 1 else (lambda i: (i, 0, 0))
    return pl.pallas_call(
        functools.partial(_s5_kernel, seq_rows=seq_rows, seqs_per_tile=seqs_per_tile, tiles_per_seq=tiles_per_seq),
        grid=(n_rows // tc,),
        in_specs=[
            pl.BlockSpec((tc, width), lambda i: (i + off, 0)),
            pl.BlockSpec((seqs_per_tile, 1, n2), seq_idx),
            pl.BlockSpec(lw["wbu"].shape, full), pl.BlockSpec(lw["a"].shape, full),
            pl.BlockSpec(lw["wc"].shape, full), pl.BlockSpec(lw["d"].shape, full),
            pl.BlockSpec(lw["wglu"].shape, full), pl.BlockSpec(lw["bglu"].shape, full),
        ],
        out_specs=[pl.BlockSpec((tc, width), lambda i: (i, 0)), pl.BlockSpec((seqs_per_tile, 1, n2), seq_idx)],
        out_shape=[jax.ShapeDtypeStruct((n_rows, width), F32), jax.ShapeDtypeStruct((n_seq, 1, n2), F32)],
        scratch_shapes=[pltpu.VMEM((tc, n2), F32), pltpu.VMEM((1, n2), F32)],
        compiler_params=_params("arbitrary"),
        name="s5",
    )(u, h0, lw["wbu"], lw["a"], lw["wc"], lw["d"], lw["wglu"], lw["bglu"])


def _pack_bf16_pairs(x):
    half = x.shape[1] // 2
    hi = pltpu.bitcast(x[:, :half].astype(BF16).astype(F32), U32)
    lo = pltpu.bitcast(x[:, half:].astype(BF16).astype(F32), U32)
    return hi | (lo >> 16)


def _unpack_bf16_pairs(p):
    hi = pltpu.bitcast(p & jnp.uint32(0xFFFF0000), F32).astype(BF16)
    lo = pltpu.bitcast(p << 16, F32).astype(BF16)
    return hi, lo


def _out_proj_kernel(x_ref, a_ref, s_ref, g_ref, woa_ref, wos_ref, wog_ref, lg_ref, lb_ref, x1_ref, xp_ref, *, alpha):
    mix = (_dot(a_ref[...].astype(BF16), woa_ref[...]) + _dot(s_ref[...].astype(BF16), wos_ref[...])
           + _dot(g_ref[...], wog_ref[...]))
    x1 = _layer_norm(alpha * x_ref[...] + mix, lg_ref[...], lb_ref[...])
    x1_ref[...] = x1
    xp_ref[...] = _pack_bf16_pairs(x1)


def _out_proj(x, attn, ssm, gm, lw, alpha):
    t, d = x.shape
    tm = ROW_TILE
    row = lambda i: (i, 0)
    full = lambda i: (0, 0)
    return pl.pallas_call(
        functools.partial(_out_proj_kernel, alpha=alpha),
        grid=(t // tm,),
        in_specs=[
            pl.BlockSpec((tm, d), row), pl.BlockSpec((tm, attn.shape[1]), row), pl.BlockSpec((tm, ssm.shape[1]), row),
            pl.BlockSpec((tm, gm.shape[1]), row),
            pl.BlockSpec(lw["woa"].shape, full), pl.BlockSpec(lw["wos"].shape, full), pl.BlockSpec(lw["wog"].shape, full),
            pl.BlockSpec(lw["ln1g"].shape, full), pl.BlockSpec(lw["ln1b"].shape, full),
        ],
        out_specs=[pl.BlockSpec((tm, d), row), pl.BlockSpec((tm, d // 2), row)],
        out_shape=[jax.ShapeDtypeStruct((t, d), F32), jax.ShapeDtypeStruct((t, d // 2), U32)],
        compiler_params=_params("parallel"),
        name="out_proj",
    )(x, attn, ssm, gm, lw["woa"], lw["wos"], lw["wog"], lw["ln1g"], lw["ln1b"])


def _first_index_of_max(x, iota, n):
    m = jnp.max(x, axis=0, keepdims=True)
    return m, jnp.min(jnp.where(x == m, iota, n), axis=0, keepdims=True)


def _router_kernel(x_ref, rwt_ref, bias_ref, idx_ref, gate_ref, rank_ref, cnt_ref, carry_scr):
    i = pl.program_id(0)

    @pl.when(i == 0)
    def _():
        carry_scr[...] = jnp.zeros(carry_scr.shape, F32)

    n_e = rwt_ref.shape[0]
    tm = x_ref.shape[0]
    per_group = n_e // N_EXPERT_GROUPS
    logits = lax.dot_general(rwt_ref[...], x_ref[...], (((1,), (1,)), ((), ())),
                             precision=lax.Precision.HIGHEST, preferred_element_type=F32)
    scores = jax.nn.sigmoid(logits)
    sel = scores + bias_ref[...]
    neg = -jnp.inf
    sub = lax.broadcasted_iota(I32, (per_group, tm), 0)
    group_scores = []
    for g in range(N_EXPERT_GROUPS):
        blk = sel[g * per_group:(g + 1) * per_group, :]
        m1, i1 = _first_index_of_max(blk, sub, per_group)
        m2 = jnp.max(jnp.where(sub == i1, neg, blk), axis=0, keepdims=True)
        group_scores.append(m1 + m2)
    work = jnp.concatenate(group_scores, axis=0)
    g_iota = lax.broadcasted_iota(I32, work.shape, 0)
    keep = jnp.zeros(work.shape, F32)
    for _ in range(TOPK_GROUPS):
        _, gi = _first_index_of_max(work, g_iota, N_EXPERT_GROUPS)
        hit = g_iota == gi
        keep = jnp.where(hit, 1.0, keep)
        work = jnp.where(hit, neg, work)
    keep_e = jnp.concatenate([jnp.broadcast_to(keep[g:g + 1, :], (per_group, tm)) for g in range(N_EXPERT_GROUPS)], axis=0)
    cand = jnp.where(keep_e > 0.0, sel, neg)
    e_iota = lax.broadcasted_iota(I32, (n_e, tm), 0)
    onehot = jnp.zeros((n_e, tm), F32)
    idx_rows, gate_rows = [], []
    for _ in range(TOP_K):
        _, ik = _first_index_of_max(cand, e_iota, n_e)
        hit = e_iota == ik
        gate_rows.append(jnp.sum(jnp.where(hit, scores, 0.0), axis=0, keepdims=True))
        onehot = jnp.where(hit, 1.0, onehot)
        cand = jnp.where(hit, neg, cand)
        idx_rows.append(ik)
    gates = jnp.concatenate(gate_rows, axis=0)
    gate_ref[...] = gates / jnp.sum(gates, axis=0, keepdims=True) * ROUTE_SCALE
    idx_ref[...] = jnp.concatenate(idx_rows, axis=0)
    before = (lax.broadcasted_iota(I32, (tm, tm), 0) < lax.broadcasted_iota(I32, (tm, tm), 1)).astype(BF16)
    prefix = _dot(onehot.astype(BF16), before) + carry_scr[:, :1]
    rank_rows = [jnp.sum(jnp.where(e_iota == ik, prefix, 0.0), axis=0, keepdims=True) for ik in idx_rows]
    rank_ref[...] = jnp.concatenate(rank_rows, axis=0).astype(I32)
    carry_scr[...] = carry_scr[...] + jnp.sum(onehot, axis=1, keepdims=True)
    cnt_ref[...] = carry_scr[...]


def _router(x1, lw):
    t, d = x1.shape
    tm = ROW_TILE
    n_e = lw["rwt"].shape[0]
    col = lambda i: (0, i)
    full = lambda i: (0, 0)
    return pl.pallas_call(
        _router_kernel,
        grid=(t // tm,),
        in_specs=[pl.BlockSpec((tm, d), lambda i: (i, 0)), pl.BlockSpec(lw["rwt"].shape, full),
                  pl.BlockSpec(lw["rbias"].shape, full)],
        out_specs=[pl.BlockSpec((TOP_K, tm), col), pl.BlockSpec((TOP_K, tm), col), pl.BlockSpec((TOP_K, tm), col),
                   pl.BlockSpec((n_e, LANES_V7X), full)],
        out_shape=[jax.ShapeDtypeStruct((TOP_K, t), I32), jax.ShapeDtypeStruct((TOP_K, t), F32),
                   jax.ShapeDtypeStruct((TOP_K, t), I32), jax.ShapeDtypeStruct((n_e, LANES_V7X), F32)],
        scratch_shapes=[pltpu.VMEM((n_e, LANES_V7X), F32)],
        compiler_params=_params("arbitrary"),
        name="router",
    )(x1, lw["rwt"], lw["rbias"])


def _dispatch_kernel(dest_ref, xp_ref, xs_in, xs_hbm, sem):
    del xs_in
    tokens = xp_ref.shape[0]

    def copy(tok, k):
        d = dest_ref[k, tok]
        return pltpu.make_async_copy(xp_ref.at[pl.ds(tok, 1), :], xs_hbm.at[pl.ds(d, 1), :], sem)

    def start(tok, carry):
        for k in range(TOP_K):
            copy(tok, k).start(priority=k % 2)
        return carry

    def wait(tok, carry):
        for k in range(TOP_K):
            copy(tok, k).wait()
        return carry

    lax.fori_loop(0, tokens, start, 0)
    lax.fori_loop(0, tokens, wait, 0)


def _dispatch(xp, dest, n_rows):
    t, w = xp.shape
    tm = MOE_TOKENS
    xs0 = jnp.zeros((n_rows, w), U32)
    return pl.pallas_call(
        _dispatch_kernel,
        grid=(t // tm,),
        in_specs=[pl.BlockSpec((TOP_K, tm), lambda i: (0, i), memory_space=pltpu.SMEM),
                  pl.BlockSpec((tm, w), lambda i: (i, 0)),
                  pl.BlockSpec(memory_space=pl.ANY)],
        out_specs=pl.BlockSpec(memory_space=pl.ANY),
        out_shape=jax.ShapeDtypeStruct((n_rows, w), U32),
        scratch_shapes=[pltpu.SemaphoreType.DMA(())],
        input_output_aliases={2: 0},
        compiler_params=_params("arbitrary"),
        name="moe_dispatch",
    )(dest, xp, xs0)


def _experts_kernel(be_ref, nb_ref, xs_ref, wg_ref, wu_ref, wd_ref, y_ref, wg_b, wu_b, wd_b):
    i = pl.program_id(0)
    fresh = jnp.logical_or(i == 0, be_ref[i] != be_ref[jnp.maximum(i - 1, 0)])

    @pl.when(fresh)
    def _():
        wg_b[...] = wg_ref[...].astype(BF16)
        wu_b[...] = wu_ref[...].astype(BF16)
        wd_b[...] = wd_ref[...].astype(BF16)

    @pl.when(i < nb_ref[0])
    def _():
        hi, lo = _unpack_bf16_pairs(xs_ref[...])
        half = hi.shape[1]
        g = _dot(hi, wg_b[:half, :]) + _dot(lo, wg_b[half:, :])
        u = _dot(hi, wu_b[:half, :]) + _dot(lo, wu_b[half:, :])
        y_ref[...] = _dot((jax.nn.silu(g) * u).astype(BF16), wd_b[...])

    @pl.when(i >= nb_ref[0])
    def _():
        y_ref[...] = jnp.zeros(y_ref.shape, F32)


def _experts(xs, block_e, n_blocks, wg, wu, wd, layer):
    n_rows, w = xs.shape
    bm = MOE_ROWS
    d, ff = wg.shape[2], wg.shape[3]
    grid_spec = pltpu.PrefetchScalarGridSpec(
        num_scalar_prefetch=2,
        grid=(n_rows // bm,),
        in_specs=[
            pl.BlockSpec((bm, w), lambda i, be, nb: (jnp.minimum(i, nb[0] - 1), 0)),
            pl.BlockSpec((None, None, d, ff), lambda i, be, nb: (layer, be[i], 0, 0)),
            pl.BlockSpec((None, None, d, ff), lambda i, be, nb: (layer, be[i], 0, 0)),
            pl.BlockSpec((None, None, ff, d), lambda i, be, nb: (layer, be[i], 0, 0)),
        ],
        out_specs=pl.BlockSpec((bm, d), lambda i, be, nb: (i, 0)),
        scratch_shapes=[pltpu.VMEM((d, ff), BF16), pltpu.VMEM((d, ff), BF16), pltpu.VMEM((ff, d), BF16)],
    )
    return pl.pallas_call(
        _experts_kernel,
        grid_spec=grid_spec,
        out_shape=jax.ShapeDtypeStruct((n_rows, d), F32),
        compiler_params=_params("arbitrary"),
        name="moe_experts",
    )(block_e, n_blocks, xs, wg, wu, wd)


def _combine_kernel(dest_ref, x1_ref, gate_ref, sg_ref, su_ref, sd_ref, lg_ref, lb_ref, yb_hbm, x2_ref, buf, sem,
                    *, alpha):
    tokens = x1_ref.shape[0]

    def copy(tok, k):
        d = dest_ref[k, tok]
        return pltpu.make_async_copy(yb_hbm.at[pl.ds(d, 1), :], buf.at[k, pl.ds(tok, 1), :], sem)

    def start(tok, carry):
        for k in range(TOP_K):
            copy(tok, k).start(priority=k % 2)
        return carry

    def wait(tok, carry):
        for k in range(TOP_K):
            copy(tok, k).wait()
        return carry

    lax.fori_loop(0, tokens, start, 0)
    x1 = x1_ref[...]
    xb = x1.astype(BF16)
    h = (jax.nn.silu(_dot(xb, sg_ref[...])) * _dot(xb, su_ref[...])).astype(BF16)
    acc = alpha * x1 + _dot(h, sd_ref[...])
    gates = jnp.concatenate([gate_ref[...], jnp.zeros((tokens - TOP_K, tokens), F32)], axis=0).T
    lax.fori_loop(0, tokens, wait, 0)
    for k in range(TOP_K):
        acc = acc + gates[:, k:k + 1] * buf[k]
    x2_ref[...] = _layer_norm(acc, lg_ref[...], lb_ref[...])


def _combine(x1, gates, dest, yb, lw, alpha):
    t, d = x1.shape
    tm = MOE_TOKENS
    full = lambda i: (0, 0)
    return pl.pallas_call(
        functools.partial(_combine_kernel, alpha=alpha),
        grid=(t // tm,),
        in_specs=[
            pl.BlockSpec((TOP_K, tm), lambda i: (0, i), memory_space=pltpu.SMEM),
            pl.BlockSpec((tm, d), lambda i: (i, 0)),
            pl.BlockSpec((TOP_K, tm), lambda i: (0, i)),
            pl.BlockSpec(lw["sg"].shape, full), pl.BlockSpec(lw["su"].shape, full), pl.BlockSpec(lw["sd"].shape, full),
            pl.BlockSpec(lw["ln2g"].shape, full), pl.BlockSpec(lw["ln2b"].shape, full),
            pl.BlockSpec(memory_space=pl.ANY),
        ],
        out_specs=pl.BlockSpec((tm, d), lambda i: (i, 0)),
        out_shape=jax.ShapeDtypeStruct((t, d), F32),
        scratch_shapes=[pltpu.VMEM((TOP_K, tm, d), F32), pltpu.SemaphoreType.DMA(())],
        compiler_params=_params("arbitrary"),
        name="moe_combine",
    )(dest, x1, gates, lw["sg"], lw["su"], lw["sd"], lw["ln2g"], lw["ln2b"], yb)


def _swap_halves(w):
    half = w.shape[-1] // 2
    return jnp.concatenate([w[..., half:], w[..., :half]], axis=-1)


def _s5_discretise(log_dt, a_re, a_im, b_re, b_im, c_re, c_im):
    n_g, n_p = a_re.shape
    n_h = b_re.shape[2]
    dt = jnp.exp(log_dt)[:, None]
    mag = jnp.exp(a_re * dt)
    ab_re, ab_im = mag * jnp.cos(a_im * dt), mag * jnp.sin(a_im * dt)
    den = jnp.square(a_re) + jnp.square(a_im)
    f_re = ((ab_re - 1.0) * a_re + ab_im * a_im) / den
    f_im = (ab_im * a_re - (ab_re - 1.0) * a_im) / den
    bb_re = f_re[..., None] * b_re - f_im[..., None] * b_im
    bb_im = f_re[..., None] * b_im + f_im[..., None] * b_re
    eye = jnp.eye(n_g, dtype=F32)
    blk = lambda m: jnp.einsum("gph,gk->ghkp", m, eye).reshape(n_g * n_h, n_g * n_p)
    wbu = jnp.concatenate([blk(bb_re), blk(bb_im)], axis=1)
    blk_c = lambda m: jnp.einsum("ghp,gk->kpgh", m, eye).reshape(n_g * n_p, n_g * n_h)
    wc = jnp.concatenate([blk_c(c_re), blk_c(-c_im)], axis=0)
    a = jnp.stack([ab_re.reshape(-1), ab_im.reshape(-1)], axis=0)
    return a, wbu.astype(BF16), wc.astype(BF16)


def _layer_weights(l, w, s_new):
    (w_in, q_norm_g, w_uq, kv_norm_g, w_uk, w_uv, ssm_log_dt, ssm_a_re, ssm_a_im, ssm_b_re, ssm_b_im, ssm_c_re,
     ssm_c_im, ssm_d, ssm_w_glu, ssm_b_glu, gmlp_ln_g, gmlp_ln_b, gmlp_w_s, gmlp_b_s, w_o, ln1_g, ln1_b, router_w,
     router_bias, sh_w_gate, sh_w_up, sh_w_down, ln2_g, ln2_b) = [a[l] for a in w]
    kv_rank, heads, nope = w_uk.shape
    q_rank = w_uq.shape[0]
    rope = w_uq.shape[1] // heads - nope
    v_dim = w_uv.shape[2]
    ssm_w = ssm_d.shape[0]
    g_heads, g_dim = gmlp_ln_g.shape
    gw = g_heads * g_dim
    chunk = gmlp_w_s.shape[1]
    o0, o1, o2, o3 = q_rank, q_rank + kv_rank, q_rank + kv_rank + rope, q_rank + kv_rank + rope + ssm_w
    w_kr = w_in[:, o1:o2]
    uq = w_uq.reshape(q_rank, heads, nope + rope)
    uq_r = uq[:, :, nope:]
    a, wbu, wc = _s5_discretise(ssm_log_dt, ssm_a_re, ssm_a_im, ssm_b_re, ssm_b_im, ssm_c_re, ssm_c_im)
    tril = jnp.tril(jnp.ones((chunk, chunk), bool))
    mix_p = jnp.where(tril, gmlp_w_s, 0.0)
    tril_s = jnp.tril(jnp.ones((s_new, s_new), bool))
    small = jnp.where(tril_s, gmlp_w_s[:, :s_new, :s_new], 0.0)
    reps = chunk // s_new
    mix_s = jnp.einsum("ab,hts->hatbs", jnp.eye(reps, dtype=F32), small).reshape(g_heads, chunk, chunk)
    bias_p = jnp.broadcast_to(gmlp_b_s.T[:, :, None], (chunk, g_heads, g_dim)).reshape(chunk, gw)
    bias_s = jnp.tile(jnp.broadcast_to(gmlp_b_s[:, :s_new].T[:, :, None], (s_new, g_heads, g_dim)).reshape(s_new, gw),
                      (reps, 1))
    attn_w = heads * v_dim
    n_e = router_w.shape[1]
    return {
        "wq": w_in[:, :o0].astype(BF16), "wkv": w_in[:, o0:o1].astype(BF16),
        "wkr": jnp.concatenate([w_kr, _swap_halves(w_kr)], axis=1).astype(BF16),
        "wu": w_in[:, o2:o3].astype(BF16), "wz": w_in[:, o3:].astype(BF16),
        "gq": q_norm_g[None, :], "gkv": kv_norm_g[None, :],
        "lng": gmlp_ln_g.reshape(1, gw), "lnb": gmlp_ln_b.reshape(1, gw),
        "mix": jnp.stack([mix_p, mix_s]).astype(BF16), "mixb": jnp.stack([bias_p, bias_s]),
        "wqn": jnp.transpose(uq[:, :, :nope], (1, 0, 2)).astype(BF16),
        "wqr": jnp.transpose(jnp.concatenate([uq_r, _swap_halves(uq_r)], axis=2), (1, 0, 2)).astype(BF16),
        "wuk": jnp.transpose(w_uk, (1, 2, 0)).astype(BF16),
        "wuv": jnp.transpose(w_uv, (1, 0, 2)).astype(BF16),
        "a": a, "wbu": wbu, "wc": wc, "d": ssm_d[None, :], "wglu": ssm_w_glu.astype(BF16), "bglu": ssm_b_glu[None, :],
        "woa": w_o[:attn_w].astype(BF16), "wos": w_o[attn_w:attn_w + ssm_w].astype(BF16),
        "wog": w_o[attn_w + ssm_w:].astype(BF16),
        "ln1g": ln1_g[None, :], "ln1b": ln1_b[None, :],
        "rwt": router_w.T, "rbias": jnp.broadcast_to(router_bias[:, None], (n_e, ROW_TILE)),
        "sg": sh_w_gate.astype(BF16), "su": sh_w_up.astype(BF16), "sd": sh_w_down.astype(BF16),
        "ln2g": ln2_g[None, :], "ln2b": ln2_b[None, :],
    }


def _rope_tables(pos, rope):
    half = rope // 2
    inv_freq = ROPE_THETA ** (-jnp.arange(half, dtype=F32) / half)
    ang = pos.astype(F32)[:, None] * inv_freq[None, :]
    cos, sin = jnp.cos(ang), jnp.sin(ang)
    return {"cos": jnp.concatenate([cos, cos], axis=1), "sin": jnp.concatenate([-sin, sin], axis=1)}


def _block_table(counts, n_rows):
    bm = MOE_ROWS
    padded = (counts + bm - 1) // bm * bm
    ends = jnp.cumsum(padded)
    n_blocks = ends[-1] // bm
    blocks = jnp.arange(n_rows // bm, dtype=I32)
    first_row = jnp.minimum(blocks, n_blocks - 1) * bm
    be = jnp.sum((ends[None, :] <= first_row[:, None]).astype(I32), axis=1)
    return ends - padded, jnp.minimum(be, counts.shape[0] - 1), n_blocks.reshape(1).astype(I32)


def _row_slots(idx, rank, offs):
    experts = jnp.arange(offs.shape[0], dtype=I32)
    return rank + jnp.sum(jnp.where(idx[..., None] == experts, offs, 0), axis=-1)


def kernel(x_prompt, x_sample, cache_kv, cache_kr, state_ssm_re, state_ssm_im, page_table, w_in, q_norm_g, w_uq, kv_norm_g, w_uk, w_uv, ssm_log_dt, ssm_a_re, ssm_a_im, ssm_b_re, ssm_b_im, ssm_c_re, ssm_c_im, ssm_d, ssm_w_glu, ssm_b_glu, gmlp_ln_g, gmlp_ln_b, gmlp_w_s, gmlp_b_s, w_o, ln1_g, ln1_b, router_w, router_bias, exp_w_gate, exp_w_up, exp_w_down, sh_w_gate, sh_w_up, sh_w_down, ln2_g, ln2_b):
    depth = w_in.shape[0]
    batch, seq, d = x_prompt.shape
    dec_b, s_new, _ = x_sample.shape
    n_pages, page = page_table.shape[1], cache_kv.shape[2]
    past = n_pages * page
    tp, ts = batch * seq, dec_b * s_new
    t = tp + ts
    kv_rank, heads, nope = w_uk.shape[1:]
    rope = cache_kr.shape[3]
    n_g, n_p = ssm_a_re.shape[1:]
    n_e = router_w.shape[2]
    assert tp % ROW_TILE == 0 and ts % ROW_TILE == 0 and seq % ROW_TILE == 0 and ROW_TILE % s_new == 0
    assert seq % KV_TILE == 0 and KV_TILE % Q_TILE == 0 and t % MOE_TOKENS == 0 and heads % Q_CHUNK_HEADS == 0
    assert n_pages % (2 * PAGE_GROUP) == 0 and KEY_CHUNK % page == 0 and (PAGE_GROUP * page) % KEY_CHUNK == 0
    assert kv_rank + rope <= QK_PAD and s_new <= LANES_V7X
    alpha = (2.0 * depth) ** 0.25
    qscale = (nope + rope) ** -0.5 * math.log2(math.e)
    n_rows = t * TOP_K + n_e * MOE_ROWS
    cache_krt = jnp.swapaxes(cache_kr, 2, 3)

    small = (w_in, q_norm_g, w_uq, kv_norm_g, w_uk, w_uv, ssm_log_dt, ssm_a_re, ssm_a_im, ssm_b_re, ssm_b_im,
             ssm_c_re, ssm_c_im, ssm_d, ssm_w_glu, ssm_b_glu, gmlp_ln_g, gmlp_ln_b, gmlp_w_s, gmlp_b_s, w_o, ln1_g,
             ln1_b, router_w, router_bias, sh_w_gate, sh_w_up, sh_w_down, ln2_g, ln2_b)
    pos = jnp.concatenate([jnp.tile(jnp.arange(seq, dtype=I32), batch),
                           jnp.tile(past + jnp.arange(s_new, dtype=I32), dec_b)])
    tabs = _rope_tables(pos, rope)
    x = jnp.concatenate([x_prompt.reshape(tp, d), x_sample.reshape(ts, d)], axis=0)
    h0_p = jnp.zeros((batch, 1, 2 * n_g * n_p), F32)
    per_layer = []
    for l in range(depth):
        lw = _layer_weights(l, small, s_new)
        cq, c, kr, kcat, u, gm, v = _in_proj(x, lw, tabs, tp // ROW_TILE)
        q_p = _mla_q(cq, lw, tabs, 0, tp, BF16, qscale)
        q_s = _mla_q(cq, lw, tabs, tp, ts, F32, qscale)
        attn_p = _attn_prompt(q_p, kcat, lw["wuv"], batch, seq)
        attn_s = _attn_sample(q_s, c, kr, tp, lw["wuv"], cache_kv, cache_krt, page_table, l)
        h0_s = jnp.concatenate([state_ssm_re[l].reshape(dec_b, 1, -1), state_ssm_im[l].reshape(dec_b, 1, -1)], axis=2)
        ssm_p, hT_p = _s5(u, h0_p, lw, 0, tp, seq)
        ssm_s, hT_s = _s5(u, h0_s, lw, tp, ts, s_new)
        attn = jnp.concatenate([attn_p, attn_s], axis=0)
        ssm = jnp.concatenate([ssm_p, ssm_s], axis=0)
        x1, xp = _out_proj(x, attn, ssm, gm, lw, alpha)
        idx, gates, rank, cnt = _router(x1, lw)
        offs, block_e, n_blocks = _block_table(cnt[:, 0].astype(I32), n_rows)
        dest = _row_slots(idx, rank, offs)
        xs = _dispatch(xp, dest, n_rows)
        yb = _experts(xs, block_e, n_blocks, exp_w_gate, exp_w_up, exp_w_down, l)
        x = _combine(x1, gates, dest, yb, lw, alpha)
        per_layer.append((c, kr, hT_p, hT_s, v))

    half = n_g * n_p
    st = lambda f: jnp.stack([f(p) for p in per_layer], axis=0)
    return (
        x[:tp].reshape(batch, seq, d),
        x[tp:].reshape(dec_b, s_new, d),
        st(lambda p: p[0][:tp].reshape(batch, seq, kv_rank)),
        st(lambda p: p[1][:tp].reshape(batch, seq, rope)),
        st(lambda p: p[2][:, 0, :half].reshape(batch, n_g, n_p)),
        st(lambda p: p[2][:, 0, half:].reshape(batch, n_g, n_p)),
        st(lambda p: p[0][tp:].reshape(dec_b, s_new, kv_rank)),
        st(lambda p: p[1][tp:].reshape(dec_b, s_new, rope)),
        st(lambda p: p[3][:, 0, :half].reshape(dec_b, n_g, n_p)),
        st(lambda p: p[3][:, 0, half:].reshape(dec_b, n_g, n_p)),
        st(lambda p: p[4][tp:].reshape(dec_b, s_new, -1)),
    )
```

```python
import functools
import math

import jax
import jax.numpy as jnp
from jax import lax
from jax.experimental import pallas as pl
from jax.experimental.pallas import tpu as pltpu

F32 = jnp.float32
BF16 = jnp.bfloat16
I32 = jnp.int32
U32 = jnp.uint32

LANES_V7X = 128
VMEM_LIMIT_V7X = 56 * 1024 * 1024

ROPE_THETA = 10000.0
EPS = 1e-6
TOP_K = 8
N_EXPERT_GROUPS = 8
TOPK_GROUPS = 4
ROUTE_SCALE = 2.5

ROW_TILE = 256
MOE_ROWS = 256
MOE_TOKENS = 128
Q_TILE = 128
KV_TILE = 256
Q_CHUNK_HEADS = 2
PAGE_GROUP = 16
PAGE_SLOTS = 4
KEY_CHUNK = 1024
QK_PAD = 384


def _params(*sem):
    return pltpu.CompilerParams(dimension_semantics=sem, vmem_limit_bytes=VMEM_LIMIT_V7X)


def _gelu(x):
    c = math.sqrt(2.0 / math.pi)
    return x * (0.5 * (1.0 + jnp.tanh(c * (x + 0.044715 * (x * x * x)))))


def _rms(x, g):
    return x * lax.rsqrt(jnp.mean(x * x, axis=-1, keepdims=True) + EPS) * g


def _layer_norm(x, g, b):
    mu = jnp.mean(x, axis=-1, keepdims=True)
    xc = x - mu
    var = jnp.mean(xc * xc, axis=-1, keepdims=True)
    return xc * lax.rsqrt(var + EPS) * g + b


def _dot(a, b):
    return jnp.dot(a, b, preferred_element_type=F32)


def _dot_nt(a, b):
    return lax.dot_general(a, b, (((1,), (1,)), ((), ())), preferred_element_type=F32)


def _in_proj_kernel(x_ref, wq_ref, wkv_ref, wkr_ref, wu_ref, wz_ref, gq_ref, gkv_ref, cos_ref, sin_ref,
                    lng_ref, lnb_ref, mix_ref, mixb_ref,
                    cq_ref, c_ref, kr_ref, kcat_ref, u_ref, gm_ref, v_ref, *, kv_rank, rope, g_heads, g_dim, chunk):
    xb = x_ref[...].astype(BF16)
    cq_ref[...] = _rms(_dot(xb, wq_ref[...]), gq_ref[...]).astype(BF16)
    c = _rms(_dot(xb, wkv_ref[...]), gkv_ref[...])
    c_ref[...] = c
    kr2 = _dot(xb, wkr_ref[...])
    kr = kr2[:, :rope] * cos_ref[...] + kr2[:, rope:] * sin_ref[...]
    kr_ref[...] = kr
    kcat_ref[:, :kv_rank] = c.astype(BF16)
    kcat_ref[:, kv_rank:kv_rank + rope] = kr.astype(BF16)
    kcat_ref[:, kv_rank + rope:] = jnp.zeros((kcat_ref.shape[0], kcat_ref.shape[1] - kv_rank - rope), BF16)
    u_ref[...] = _dot(xb, wu_ref[...])
    z = _gelu(_dot(xb, wz_ref[...]))
    width = g_heads * g_dim
    for h in range(g_heads):
        sl = slice(h * g_dim, (h + 1) * g_dim)
        vh = _layer_norm(z[:, width + h * g_dim: width + (h + 1) * g_dim], lng_ref[:, sl], lnb_ref[:, sl])
        v_ref[:, sl] = vh
        vb = vh.astype(BF16)
        for j in range(x_ref.shape[0] // chunk):
            rows = slice(j * chunk, (j + 1) * chunk)
            sv = _dot(mix_ref[h], vb[rows]) + mixb_ref[:, sl]
            gm_ref[rows, sl] = (z[rows, sl] * sv).astype(BF16)


def _in_proj(x, lw, tabs, n_prompt_tiles):
    t, d = x.shape
    tm = ROW_TILE
    kv_rank, rope = lw["wkv"].shape[1], lw["wkr"].shape[1] // 2
    gw = lw["lng"].shape[1]
    g_heads, chunk = lw["mix"].shape[1], lw["mix"].shape[2]
    row = lambda i: (i, 0)
    full = lambda i: (0, 0)
    trunk = lambda i: (jnp.where(i >= n_prompt_tiles, 1, 0), 0, 0, 0)
    trunk3 = lambda i: (jnp.where(i >= n_prompt_tiles, 1, 0), 0, 0)
    kern = functools.partial(_in_proj_kernel, kv_rank=kv_rank, rope=rope, g_heads=g_heads, g_dim=gw // g_heads,
                             chunk=chunk)
    q_rank, ssm_w = lw["wq"].shape[1], lw["wu"].shape[1]
    return pl.pallas_call(
        kern,
        grid=(t // tm,),
        in_specs=[
            pl.BlockSpec((tm, d), row),
            pl.BlockSpec(lw["wq"].shape, full), pl.BlockSpec(lw["wkv"].shape, full),
            pl.BlockSpec(lw["wkr"].shape, full), pl.BlockSpec(lw["wu"].shape, full),
            pl.BlockSpec(lw["wz"].shape, full),
            pl.BlockSpec(lw["gq"].shape, full), pl.BlockSpec(lw["gkv"].shape, full),
            pl.BlockSpec((tm, rope), row), pl.BlockSpec((tm, rope), row),
            pl.BlockSpec(lw["lng"].shape, full), pl.BlockSpec(lw["lnb"].shape, full),
            pl.BlockSpec((None,) + lw["mix"].shape[1:], trunk),
            pl.BlockSpec((None,) + lw["mixb"].shape[1:], trunk3),
        ],
        out_specs=[
            pl.BlockSpec((tm, q_rank), row), pl.BlockSpec((tm, kv_rank), row), pl.BlockSpec((tm, rope), row),
            pl.BlockSpec((tm, QK_PAD), row), pl.BlockSpec((tm, ssm_w), row), pl.BlockSpec((tm, gw), row),
            pl.BlockSpec((tm, gw), row),
        ],
        out_shape=[
            jax.ShapeDtypeStruct((t, q_rank), BF16), jax.ShapeDtypeStruct((t, kv_rank), F32),
            jax.ShapeDtypeStruct((t, rope), F32), jax.ShapeDtypeStruct((t, QK_PAD), BF16),
            jax.ShapeDtypeStruct((t, ssm_w), F32), jax.ShapeDtypeStruct((t, gw), BF16),
            jax.ShapeDtypeStruct((t, gw), F32),
        ],
        compiler_params=_params("parallel"),
        name="in_proj",
    )(x, lw["wq"], lw["wkv"], lw["wkr"], lw["wu"], lw["wz"], lw["gq"], lw["gkv"], tabs["cos"], tabs["sin"],
      lw["lng"], lw["lnb"], lw["mix"], lw["mixb"])


def _mla_q_kernel(cq_ref, wqn_ref, wqr_ref, wuk_ref, cos_ref, sin_ref, q_ref, *, kv_rank, rope, qscale):
    cq = cq_ref[...]
    for h in range(q_ref.shape[0]):
        qn = _dot(cq, wqn_ref[h]).astype(BF16)
        q_ref[h, :, :kv_rank] = (_dot(qn, wuk_ref[h]) * qscale).astype(q_ref.dtype)
        qr2 = _dot(cq, wqr_ref[h])
        qr = qr2[:, :rope] * cos_ref[...] + qr2[:, rope:] * sin_ref[...]
        q_ref[h, :, kv_rank:kv_rank + rope] = (qr * qscale).astype(q_ref.dtype)
        q_ref[h, :, kv_rank + rope:] = jnp.zeros((q_ref.shape[1], q_ref.shape[2] - kv_rank - rope), q_ref.dtype)


def _mla_q(cq, lw, tabs, row0, n_rows, out_dtype, qscale):
    tm = ROW_TILE
    heads, q_rank, _ = lw["wqn"].shape
    kv_rank, rope = lw["wuk"].shape[2], lw["wqr"].shape[2] // 2
    off = row0 // tm
    row = lambda i: (i + off, 0)
    full = lambda i: (0, 0, 0)
    return pl.pallas_call(
        functools.partial(_mla_q_kernel, kv_rank=kv_rank, rope=rope, qscale=qscale),
        grid=(n_rows // tm,),
        in_specs=[
            pl.BlockSpec((tm, q_rank), row),
            pl.BlockSpec(lw["wqn"].shape, full), pl.BlockSpec(lw["wqr"].shape, full),
            pl.BlockSpec(lw["wuk"].shape, full),
            pl.BlockSpec((tm, rope), row), pl.BlockSpec((tm, rope), row),
        ],
        out_specs=pl.BlockSpec((heads, tm, QK_PAD), lambda i: (0, i, 0)),
        out_shape=jax.ShapeDtypeStruct((heads, n_rows, QK_PAD), out_dtype),
        compiler_params=_params("parallel"),
        name="mla_q",
    )(cq, lw["wqn"], lw["wqr"], lw["wuk"], tabs["cos"], tabs["sin"])


def _lane_tiles(x, op):
    tiles = [x[:, i * LANES_V7X:(i + 1) * LANES_V7X] for i in range(x.shape[1] // LANES_V7X)]
    return functools.reduce(op, tiles)


def _widen(x, width):
    return jnp.tile(x, (1, width // LANES_V7X))


def _attn_prompt_kernel(q_ref, k_ref, wuv_ref, o_ref, s_scr, m_scr, l_scr, acc_scr, *, tq, tk, kv_rank, hpc):
    qi = pl.program_id(1)
    heads = q_ref.shape[0]
    cr = hpc * tq
    q = q_ref[...].reshape(heads * tq, q_ref.shape[2])
    m_scr[...] = jnp.full(m_scr.shape, -jnp.inf, F32)
    l_scr[...] = jnp.zeros(l_scr.shape, F32)
    acc_scr[...] = jnp.zeros(acc_scr.shape, F32)
    n_full = (qi * tq) // tk

    def keys(j):
        return k_ref[pl.ds(pl.multiple_of(j * tk, tk), tk), :]

    def scores(j, masked):
        s = _dot_nt(q, keys(j))
        if masked:
            qpos = qi * tq + lax.broadcasted_iota(I32, s.shape, 0) % tq
            kpos = j * tk + lax.broadcasted_iota(I32, s.shape, 1)
            s = jnp.where(kpos <= qpos, s, -jnp.inf)
        s_scr[j % 2] = s

    def weigh(j):
        v = keys(j)[:, :kv_rank]
        for c in range(heads // hpc):
            rows = slice(c * cr, (c + 1) * cr)
            s = s_scr[j % 2, rows]
            m_prev = m_scr[rows]
            m_new = jnp.maximum(m_prev, jnp.max(s, axis=-1, keepdims=True))
            alpha = jnp.exp2(m_prev - m_new)
            p = jnp.exp2(s - _widen(m_new, tk))
            l_scr[rows] = alpha * l_scr[rows] + _lane_tiles(p, jnp.add)
            acc_scr[rows] = _widen(alpha, kv_rank) * acc_scr[rows] + _dot(p.astype(BF16), v)
            m_scr[rows] = m_new

    @pl.when(n_full == 0)
    def _():
        scores(0, True)

    @pl.when(n_full > 0)
    def _():
        scores(0, False)

        def step(j, carry):
            weigh(j)
            scores(j + 1, False)
            return carry

        lax.fori_loop(0, n_full - 1, step, 0)
        weigh(n_full - 1)
        scores(n_full, True)

    weigh(n_full)
    o = (acc_scr[...] / jnp.sum(l_scr[...], axis=-1, keepdims=True)).astype(BF16)
    v_dim = wuv_ref.shape[2]
    for h in range(heads):
        o_ref[:, h * v_dim:(h + 1) * v_dim] = _dot(o[h * tq:(h + 1) * tq], wuv_ref[h])


def _attn_prompt(q, kcat, wuv, batch, seq):
    heads, tp, _ = q.shape
    kv_rank, v_dim = wuv.shape[1], wuv.shape[2]
    tq, tk = Q_TILE, KV_TILE
    nq = seq // tq
    rows = heads * tq
    return pl.pallas_call(
        functools.partial(_attn_prompt_kernel, tq=tq, tk=tk, kv_rank=kv_rank, hpc=Q_CHUNK_HEADS),
        grid=(batch, nq),
        in_specs=[
            pl.BlockSpec((heads, tq, QK_PAD), lambda b, i: (0, b * nq + i, 0)),
            pl.BlockSpec((seq, QK_PAD), lambda b, i: (b, 0)),
            pl.BlockSpec(wuv.shape, lambda b, i: (0, 0, 0)),
        ],
        out_specs=pl.BlockSpec((tq, heads * v_dim), lambda b, i: (b * nq + i, 0)),
        out_shape=jax.ShapeDtypeStruct((tp, heads * v_dim), F32),
        scratch_shapes=[pltpu.VMEM((2, rows, tk), F32), pltpu.VMEM((rows, LANES_V7X), F32),
                        pltpu.VMEM((rows, LANES_V7X), F32), pltpu.VMEM((rows, kv_rank), F32)],
        compiler_params=_params("parallel", "arbitrary"),
        name="attn_prompt",
    )(q, kcat, wuv)


def _attn_sample_kernel(pt_ref, q_ref, cnew_ref, krnew_ref, wuv_ref, ckv_hbm, ckrt_hbm, o_ref,
                        kvbuf, krbuf, kb_all, krt_scr, knew_scr, s_scr, sems,
                        *, layer, n_pages, group, page, kv_rank, rope, chunk):
    b = pl.program_id(0)
    heads, s_new, _ = q_ref.shape
    rows = heads * s_new
    n_groups = n_pages // group
    ppc = chunk // page
    cpg = group // ppc
    q = q_ref[...].reshape(rows, q_ref.shape[2]).astype(BF16)
    q_lat, q_rope = q[:, :kv_rank], q[:, kv_rank:]

    n_slots = kvbuf.shape[0]
    ahead = n_slots - 1
    total = pl.num_programs(0) * n_groups

    def copies(gg, slot, p):
        pg = pt_ref[gg * group + p]
        return (pltpu.make_async_copy(ckv_hbm.at[layer, pg], kvbuf.at[slot, p], sems.at[0, slot]),
                pltpu.make_async_copy(ckrt_hbm.at[layer, pg], krbuf.at[slot, p], sems.at[1, slot]))

    def start(gg):
        for p in range(group):
            for cp in copies(gg, gg % n_slots, p):
                cp.start()

    def wait(slot):
        for p in range(group):
            for cp in copies(0, slot, p):
                cp.wait()

    @pl.when(b == 0)
    def _():
        for gg in range(ahead):
            start(gg)

    krt_scr[:, rope:, :] = jnp.zeros((cpg, krt_scr.shape[1] - rope, chunk), BF16)

    def stream(g, m_part):
        gg = b * n_groups + g
        slot = gg % n_slots

        @pl.when(gg + ahead < total)
        def _():
            start(gg + ahead)

        wait(slot)
        for c in range(cpg):
            kvb = kvbuf[slot, c * ppc:(c + 1) * ppc].reshape(chunk, kv_rank).astype(BF16)
            kb_all[g * cpg + c] = kvb
            for p in range(ppc):
                krt_scr[c, :rope, p * page:(p + 1) * page] = krbuf[slot, c * ppc + p].astype(BF16)
            s = _dot_nt(q_lat, kvb) + _dot(q_rope, krt_scr[c])
            s_scr[g * cpg + c] = s
            m_part = jnp.maximum(m_part, _lane_tiles(s, jnp.maximum))
        return m_part

    m_part = lax.fori_loop(0, n_groups, stream, jnp.full((rows, LANES_V7X), -jnp.inf, F32))

    knew_scr[...] = jnp.zeros(knew_scr.shape, BF16)
    knew_scr[:s_new, :kv_rank] = cnew_ref[...].astype(BF16)
    knew_scr[:s_new, kv_rank:kv_rank + rope] = krnew_ref[...].astype(BF16)
    kn = knew_scr[...]
    sn = _dot_nt(q, kn)
    qpos = lax.broadcasted_iota(I32, sn.shape, 0) % s_new
    kpos = lax.broadcasted_iota(I32, sn.shape, 1)
    sn = jnp.where(kpos <= qpos, sn, -jnp.inf)
    m = jnp.max(jnp.maximum(m_part, sn), axis=-1, keepdims=True)
    m_wide = jnp.broadcast_to(m, (rows, chunk))

    def weigh(i, carry):
        l_part, acc = carry
        p = jnp.exp2(s_scr[i] - m_wide)
        return l_part + _lane_tiles(p, jnp.add), acc + _dot(p.astype(BF16), kb_all[i])

    pn = jnp.exp2(sn - m_wide[:, :LANES_V7X])
    l_part, acc = lax.fori_loop(0, n_groups * cpg, weigh, (pn, _dot(pn.astype(BF16), kn[:, :kv_rank])), unroll=2)
    o = (acc / jnp.sum(l_part, axis=-1, keepdims=True)).astype(BF16)
    v_dim = wuv_ref.shape[2]
    for h in range(heads):
        o_ref[:, h * v_dim:(h + 1) * v_dim] = _dot(o, wuv_ref[h])[h * s_new:(h + 1) * s_new]


def _attn_sample(q, c, kr, row0, wuv, cache_kv, cache_krt, page_table, layer):
    heads, ts, _ = q.shape
    dec_b, n_pages = page_table.shape
    s_new = ts // dec_b
    page, kv_rank = cache_kv.shape[2], cache_kv.shape[3]
    rope = cache_krt.shape[2]
    v_dim = wuv.shape[2]
    group, chunk = PAGE_GROUP, KEY_CHUNK
    n_chunks = n_pages * page // chunk
    rows = heads * s_new
    off = row0 // s_new
    grid_spec = pltpu.PrefetchScalarGridSpec(
        num_scalar_prefetch=1,
        grid=(dec_b,),
        in_specs=[
            pl.BlockSpec((heads, s_new, QK_PAD), lambda b, pt: (0, b, 0)),
            pl.BlockSpec((s_new, kv_rank), lambda b, pt: (b + off, 0)),
            pl.BlockSpec((s_new, rope), lambda b, pt: (b + off, 0)),
            pl.BlockSpec(wuv.shape, lambda b, pt: (0, 0, 0)),
            pl.BlockSpec(memory_space=pl.ANY),
            pl.BlockSpec(memory_space=pl.ANY),
        ],
        out_specs=pl.BlockSpec((s_new, heads * v_dim), lambda b, pt: (b, 0)),
        scratch_shapes=[
            pltpu.VMEM((PAGE_SLOTS, group, page, kv_rank), F32),
            pltpu.VMEM((PAGE_SLOTS, group, rope, page), F32),
            pltpu.VMEM((n_chunks, chunk, kv_rank), BF16),
            pltpu.VMEM((group * page // chunk, QK_PAD - kv_rank, chunk), BF16),
            pltpu.VMEM((LANES_V7X, QK_PAD), BF16),
            pltpu.VMEM((n_chunks, rows, chunk), F32),
            pltpu.SemaphoreType.DMA((2, PAGE_SLOTS)),
        ],
    )
    return pl.pallas_call(
        functools.partial(_attn_sample_kernel, layer=layer, n_pages=n_pages, group=group, page=page,
                          kv_rank=kv_rank, rope=rope, chunk=chunk),
        grid_spec=grid_spec,
        out_shape=jax.ShapeDtypeStruct((ts, heads * v_dim), F32),
        compiler_params=_params("arbitrary"),
        name="attn_sample",
    )(page_table.reshape(-1), q, c, kr, wuv, cache_kv, cache_krt)


def _s5_kernel(u_ref, h0_ref, wbu_ref, a_ref, wc_ref, d_ref, wglu_ref, bglu_ref, y_ref, ht_ref, hs_scr, carry_scr,
               *, seq_rows, seqs_per_tile, tiles_per_seq):
    i = pl.program_id(0)
    u = u_ref[...]
    hs_scr[...] = _dot(u.astype(BF16), wbu_ref[...])
    n = a_ref.shape[1]
    a_re, a_im = a_ref[0:1, :], a_ref[1:2, :]

    if tiles_per_seq > 1:
        @pl.when(i % tiles_per_seq == 0)
        def _():
            carry_scr[...] = h0_ref[0]

    def one_sequence(s, carry):
        h = carry_scr[...] if tiles_per_seq > 1 else h0_ref[s]

        def row(t, hc):
            hr, hi = hc
            r = s * seq_rows + t
            nr = a_re * hr - a_im * hi + hs_scr[pl.ds(r, 1), :n]
            ni = a_re * hi + a_im * hr + hs_scr[pl.ds(r, 1), n:]
            hs_scr[pl.ds(r, 1), :n] = nr
            hs_scr[pl.ds(r, 1), n:] = ni
            return nr, ni

        hr, hi = lax.fori_loop(0, seq_rows, row, (h[:, :n], h[:, n:]), unroll=8)
        last = jnp.concatenate([hr, hi], axis=1)
        ht_ref[s] = last
        if tiles_per_seq > 1:
            carry_scr[...] = last
        return carry

    lax.fori_loop(0, seqs_per_tile, one_sequence, 0)
    y = _dot(hs_scr[...].astype(BF16), wc_ref[...]) + d_ref[...] * u
    g = _gelu(y)
    y_ref[...] = g * jax.nn.sigmoid(_dot(g.astype(BF16), wglu_ref[...]) + bglu_ref[...])


def _s5(u, h0, lw, row0, n_rows, seq_len):
    tc = ROW_TILE
    width = u.shape[1]
    n2 = lw["wbu"].shape[1]
    seq_rows = min(seq_len, tc)
    seqs_per_tile = tc // seq_rows
    tiles_per_seq = max(1, seq_len // tc)
    n_seq = n_rows // seq_len
    off = row0 // tc
    full = lambda i: (0, 0)
    seq_idx = (lambda i: (i // tiles_per_seq, 0, 0)) if tiles_per_seq > 1 else (lambda i: (i, 0, 0))
    return pl.pallas_call(
        functools.partial(_s5_kernel, seq_rows=seq_rows, seqs_per_tile=seqs_per_tile, tiles_per_seq=tiles_per_seq),
        grid=(n_rows // tc,),
        in_specs=[
            pl.BlockSpec((tc, width), lambda i: (i + off, 0)),
            pl.BlockSpec((seqs_per_tile, 1, n2), seq_idx),
            pl.BlockSpec(lw["wbu"].shape, full), pl.BlockSpec(lw["a"].shape, full),
            pl.BlockSpec(lw["wc"].shape, full), pl.BlockSpec(lw["d"].shape, full),
            pl.BlockSpec(lw["wglu"].shape, full), pl.BlockSpec(lw["bglu"].shape, full),
        ],
        out_specs=[pl.BlockSpec((tc, width), lambda i: (i, 0)), pl.BlockSpec((seqs_per_tile, 1, n2), seq_idx)],
        out_shape=[jax.ShapeDtypeStruct((n_rows, width), F32), jax.ShapeDtypeStruct((n_seq, 1, n2), F32)],
        scratch_shapes=[pltpu.VMEM((tc, n2), F32), pltpu.VMEM((1, n2), F32)],
        compiler_params=_params("arbitrary"),
        name="s5",
    )(u, h0, lw["wbu"], lw["a"], lw["wc"], lw["d"], lw["wglu"], lw["bglu"])


def _pack_bf16_pairs(x):
    half = x.shape[1] // 2
    hi = pltpu.bitcast(x[:, :half].astype(BF16).astype(F32), U32)
    lo = pltpu.bitcast(x[:, half:].astype(BF16).astype(F32), U32)
    return hi | (lo >> 16)


def _unpack_bf16_pairs(p):
    hi = pltpu.bitcast(p & jnp.uint32(0xFFFF0000), F32).astype(BF16)
    lo = pltpu.bitcast(p << 16, F32).astype(BF16)
    return hi, lo


def _out_proj_kernel(x_ref, ap_ref, as_ref, sp_ref, ss_ref, g_ref, woa_ref, wos_ref, wog_ref, lg_ref, lb_ref,
                     x1_ref, xp_ref, *, alpha, n_prompt_tiles):
    prompt = pl.program_id(0) < n_prompt_tiles
    a = jnp.where(prompt, ap_ref[...], as_ref[...])
    s = jnp.where(prompt, sp_ref[...], ss_ref[...])
    mix = _dot(a.astype(BF16), woa_ref[...]) + _dot(s.astype(BF16), wos_ref[...]) + _dot(g_ref[...], wog_ref[...])
    x1 = _layer_norm(alpha * x_ref[...] + mix, lg_ref[...], lb_ref[...])
    x1_ref[...] = x1
    xp_ref[...] = _pack_bf16_pairs(x1)


def _out_proj(x, attn_p, attn_s, ssm_p, ssm_s, gm, lw, alpha):
    t, d = x.shape
    tm = ROW_TILE
    n_p, n_s = attn_p.shape[0] // tm, attn_s.shape[0] // tm
    row = lambda i: (i, 0)
    full = lambda i: (0, 0)
    row_p = lambda i: (jnp.minimum(i, n_p - 1), 0)
    row_s = lambda i: (jnp.clip(i - n_p, 0, n_s - 1), 0)
    return pl.pallas_call(
        functools.partial(_out_proj_kernel, alpha=alpha, n_prompt_tiles=n_p),
        grid=(t // tm,),
        in_specs=[
            pl.BlockSpec((tm, d), row),
            pl.BlockSpec((tm, attn_p.shape[1]), row_p), pl.BlockSpec((tm, attn_s.shape[1]), row_s),
            pl.BlockSpec((tm, ssm_p.shape[1]), row_p), pl.BlockSpec((tm, ssm_s.shape[1]), row_s),
            pl.BlockSpec((tm, gm.shape[1]), row),
            pl.BlockSpec(lw["woa"].shape, full), pl.BlockSpec(lw["wos"].shape, full), pl.BlockSpec(lw["wog"].shape, full),
            pl.BlockSpec(lw["ln1g"].shape, full), pl.BlockSpec(lw["ln1b"].shape, full),
        ],
        out_specs=[pl.BlockSpec((tm, d), row), pl.BlockSpec((tm, d // 2), row)],
        out_shape=[jax.ShapeDtypeStruct((t, d), F32), jax.ShapeDtypeStruct((t, d // 2), U32)],
        compiler_params=_params("parallel"),
        name="out_proj",
    )(x, attn_p, attn_s, ssm_p, ssm_s, gm, lw["woa"], lw["wos"], lw["wog"], lw["ln1g"], lw["ln1b"])


def _first_index_of_max(x, iota, n):
    m = jnp.max(x, axis=0, keepdims=True)
    return m, jnp.min(jnp.where(x == m, iota, n), axis=0, keepdims=True)


def _router_kernel(x_ref, rwt_ref, bias_ref, idx_ref, gate_ref, rank_ref, cnt_ref, carry_scr):
    i = pl.program_id(0)

    @pl.when(i == 0)
    def _():
        carry_scr[...] = jnp.zeros(carry_scr.shape, F32)

    n_e = rwt_ref.shape[0]
    tm = x_ref.shape[0]
    per_group = n_e // N_EXPERT_GROUPS
    logits = lax.dot_general(rwt_ref[...], x_ref[...], (((1,), (1,)), ((), ())),
                             precision=lax.Precision.HIGHEST, preferred_element_type=F32)
    scores = jax.nn.sigmoid(logits)
    sel = scores + bias_ref[...]
    neg = -jnp.inf
    sub = lax.broadcasted_iota(I32, (per_group, tm), 0)
    group_scores = []
    for g in range(N_EXPERT_GROUPS):
        blk = sel[g * per_group:(g + 1) * per_group, :]
        m1, i1 = _first_index_of_max(blk, sub, per_group)
        m2 = jnp.max(jnp.where(sub == i1, neg, blk), axis=0, keepdims=True)
        group_scores.append(m1 + m2)
    work = jnp.concatenate(group_scores, axis=0)
    g_iota = lax.broadcasted_iota(I32, work.shape, 0)
    keep = jnp.zeros(work.shape, F32)
    for _ in range(TOPK_GROUPS):
        _, gi = _first_index_of_max(work, g_iota, N_EXPERT_GROUPS)
        hit = g_iota == gi
        keep = jnp.where(hit, 1.0, keep)
        work = jnp.where(hit, neg, work)
    keep_e = jnp.concatenate([jnp.broadcast_to(keep[g:g + 1, :], (per_group, tm)) for g in range(N_EXPERT_GROUPS)], axis=0)
    cand = jnp.where(keep_e > 0.0, sel, neg)
    e_iota = lax.broadcasted_iota(I32, (n_e, tm), 0)
    onehot = jnp.zeros((n_e, tm), F32)
    idx_rows, gate_rows = [], []
    for _ in range(TOP_K):
        _, ik = _first_index_of_max(cand, e_iota, n_e)
        hit = e_iota == ik
        gate_rows.append(jnp.sum(jnp.where(hit, scores, 0.0), axis=0, keepdims=True))
        onehot = jnp.where(hit, 1.0, onehot)
        cand = jnp.where(hit, neg, cand)
        idx_rows.append(ik)
    gates = jnp.concatenate(gate_rows, axis=0)
    gate_ref[...] = gates / jnp.sum(gates, axis=0, keepdims=True) * ROUTE_SCALE
    idx_ref[...] = jnp.concatenate(idx_rows, axis=0)
    before = (lax.broadcasted_iota(I32, (tm, tm), 0) < lax.broadcasted_iota(I32, (tm, tm), 1)).astype(BF16)
    prefix = _dot(onehot.astype(BF16), before) + carry_scr[:, :1]
    rank_rows = [jnp.sum(jnp.where(e_iota == ik, prefix, 0.0), axis=0, keepdims=True) for ik in idx_rows]
    rank_ref[...] = jnp.concatenate(rank_rows, axis=0).astype(I32)
    carry_scr[...] = carry_scr[...] + jnp.sum(onehot, axis=1, keepdims=True)
    cnt_ref[...] = carry_scr[...]


def _router(x1, lw):
    t, d = x1.shape
    tm = ROW_TILE
    n_e = lw["rwt"].shape[0]
    col = lambda i: (0, i)
    full = lambda i: (0, 0)
    return pl.pallas_call(
        _router_kernel,
        grid=(t // tm,),
        in_specs=[pl.BlockSpec((tm, d), lambda i: (i, 0)), pl.BlockSpec(lw["rwt"].shape, full),
                  pl.BlockSpec(lw["rbias"].shape, full)],
        out_specs=[pl.BlockSpec((TOP_K, tm), col), pl.BlockSpec((TOP_K, tm), col), pl.BlockSpec((TOP_K, tm), col),
                   pl.BlockSpec((n_e, LANES_V7X), full)],
        out_shape=[jax.ShapeDtypeStruct((TOP_K, t), I32), jax.ShapeDtypeStruct((TOP_K, t), F32),
                   jax.ShapeDtypeStruct((TOP_K, t), I32), jax.ShapeDtypeStruct((n_e, LANES_V7X), F32)],
        scratch_shapes=[pltpu.VMEM((n_e, LANES_V7X), F32)],
        compiler_params=_params("arbitrary"),
        name="router",
    )(x1, lw["rwt"], lw["rbias"])


def _dispatch_kernel(dest_ref, xp_ref, xs_in, xs_hbm, sem):
    del xs_in
    tokens = xp_ref.shape[0]

    def copy(tok, k):
        d = dest_ref[k, tok]
        return pltpu.make_async_copy(xp_ref.at[pl.ds(tok, 1), :], xs_hbm.at[pl.ds(d, 1), :], sem)

    def start(tok, carry):
        for k in range(TOP_K):
            copy(tok, k).start(priority=k % 2)
        return carry

    def wait(tok, carry):
        for k in range(TOP_K):
            copy(tok, k).wait()
        return carry

    lax.fori_loop(0, tokens, start, 0)
    lax.fori_loop(0, tokens, wait, 0)


def _dispatch(xp, dest, n_rows):
    t, w = xp.shape
    tm = MOE_TOKENS
    xs0 = jnp.zeros((n_rows, w), U32)
    return pl.pallas_call(
        _dispatch_kernel,
        grid=(t // tm,),
        in_specs=[pl.BlockSpec((TOP_K, tm), lambda i: (0, i), memory_space=pltpu.SMEM),
                  pl.BlockSpec((tm, w), lambda i: (i, 0)),
                  pl.BlockSpec(memory_space=pl.ANY)],
        out_specs=pl.BlockSpec(memory_space=pl.ANY),
        out_shape=jax.ShapeDtypeStruct((n_rows, w), U32),
        scratch_shapes=[pltpu.SemaphoreType.DMA(())],
        input_output_aliases={2: 0},
        compiler_params=_params("arbitrary"),
        name="moe_dispatch",
    )(dest, xp, xs0)


def _experts_kernel(be_ref, nb_ref, nxt_ref, xs_ref, wg_hbm, wu_hbm, wd_hbm, y_ref, wg_f, wu_f, wd_f, wg_b, wu_b, wd_b,
                    sems, *, layer):
    i = pl.program_id(0)
    e = be_ref[i]
    fresh = jnp.logical_or(i == 0, e != be_ref[jnp.maximum(i - 1, 0)])

    def copies(ex):
        return (pltpu.make_async_copy(wg_hbm.at[layer, ex], wg_f, sems.at[0]),
                pltpu.make_async_copy(wu_hbm.at[layer, ex], wu_f, sems.at[1]),
                pltpu.make_async_copy(wd_hbm.at[layer, ex], wd_f, sems.at[2]))

    @pl.when(i == 0)
    def _():
        for cp in copies(e):
            cp.start()

    @pl.when(fresh)
    def _():
        for cp in copies(e):
            cp.wait()
        wg_b[...] = wg_f[...].astype(BF16)
        wu_b[...] = wu_f[...].astype(BF16)
        wd_b[...] = wd_f[...].astype(BF16)

        @pl.when(nxt_ref[e] >= 0)
        def _():
            for cp in copies(nxt_ref[e]):
                cp.start()

    @pl.when(i < nb_ref[0])
    def _():
        hi, lo = _unpack_bf16_pairs(xs_ref[...])
        half = hi.shape[1]
        g = _dot(hi, wg_b[:half, :]) + _dot(lo, wg_b[half:, :])
        u = _dot(hi, wu_b[:half, :]) + _dot(lo, wu_b[half:, :])
        y_ref[...] = _dot((jax.nn.silu(g) * u).astype(BF16), wd_b[...])

    @pl.when(i >= nb_ref[0])
    def _():
        y_ref[...] = jnp.zeros(y_ref.shape, F32)


def _experts(xs, block_e, n_blocks, next_e, wg, wu, wd, layer):
    n_rows, w = xs.shape
    bm = MOE_ROWS
    d, ff = wg.shape[2], wg.shape[3]
    grid_spec = pltpu.PrefetchScalarGridSpec(
        num_scalar_prefetch=3,
        grid=(n_rows // bm,),
        in_specs=[
            pl.BlockSpec((bm, w), lambda i, be, nb, nx: (jnp.minimum(i, nb[0] - 1), 0)),
            pl.BlockSpec(memory_space=pl.ANY), pl.BlockSpec(memory_space=pl.ANY), pl.BlockSpec(memory_space=pl.ANY),
        ],
        out_specs=pl.BlockSpec((bm, d), lambda i, be, nb, nx: (i, 0)),
        scratch_shapes=[pltpu.VMEM((d, ff), F32), pltpu.VMEM((d, ff), F32), pltpu.VMEM((ff, d), F32),
                        pltpu.VMEM((d, ff), BF16), pltpu.VMEM((d, ff), BF16), pltpu.VMEM((ff, d), BF16),
                        pltpu.SemaphoreType.DMA((3,))],
    )
    return pl.pallas_call(
        functools.partial(_experts_kernel, layer=layer),
        grid_spec=grid_spec,
        out_shape=jax.ShapeDtypeStruct((n_rows, d), F32),
        compiler_params=_params("arbitrary"),
        name="moe_experts",
    )(block_e, n_blocks, next_e, xs, wg, wu, wd)


def _combine_kernel(dest_ref, dnext_ref, x1_ref, gate_ref, sg_ref, su_ref, sd_ref, lg_ref, lb_ref, yb_hbm, x2_ref,
                    buf, sems, *, alpha):
    i = pl.program_id(0)
    tokens = x1_ref.shape[0]
    slot = i % 2

    def copy(dref, s, tok, k):
        d = dref[k, tok]
        return pltpu.make_async_copy(yb_hbm.at[pl.ds(d, 1), :], buf.at[s, k, pl.ds(tok, 1), :], sems.at[s])

    def gather(dref, s):
        def start(tok, carry):
            for k in range(TOP_K):
                copy(dref, s, tok, k).start()
            return carry

        lax.fori_loop(0, tokens, start, 0)

    def wait(tok, carry):
        for k in range(TOP_K):
            copy(dest_ref, slot, tok, k).wait()
        return carry

    @pl.when(i == 0)
    def _():
        gather(dest_ref, 0)

    @pl.when(i + 1 < pl.num_programs(0))
    def _():
        gather(dnext_ref, 1 - slot)

    x1 = x1_ref[...]
    xb = x1.astype(BF16)
    h = (jax.nn.silu(_dot(xb, sg_ref[...])) * _dot(xb, su_ref[...])).astype(BF16)
    acc = alpha * x1 + _dot(h, sd_ref[...])
    gates = jnp.concatenate([gate_ref[...], jnp.zeros((tokens - TOP_K, tokens), F32)], axis=0).T
    lax.fori_loop(0, tokens, wait, 0)
    for k in range(TOP_K):
        acc = acc + gates[:, k:k + 1] * buf[slot, k]
    x2_ref[...] = _layer_norm(acc, lg_ref[...], lb_ref[...])


def _combine(x1, gates, dest, yb, lw, alpha):
    t, d = x1.shape
    tm = MOE_TOKENS
    n = t // tm
    full = lambda i: (0, 0)
    return pl.pallas_call(
        functools.partial(_combine_kernel, alpha=alpha),
        grid=(n,),
        in_specs=[
            pl.BlockSpec((TOP_K, tm), lambda i: (0, i), memory_space=pltpu.SMEM),
            pl.BlockSpec((TOP_K, tm), lambda i: (0, jnp.minimum(i + 1, n - 1)), memory_space=pltpu.SMEM),
            pl.BlockSpec((tm, d), lambda i: (i, 0)),
            pl.BlockSpec((TOP_K, tm), lambda i: (0, i)),
            pl.BlockSpec(lw["sg"].shape, full), pl.BlockSpec(lw["su"].shape, full), pl.BlockSpec(lw["sd"].shape, full),
            pl.BlockSpec(lw["ln2g"].shape, full), pl.BlockSpec(lw["ln2b"].shape, full),
            pl.BlockSpec(memory_space=pl.ANY),
        ],
        out_specs=pl.BlockSpec((tm, d), lambda i: (i, 0)),
        out_shape=jax.ShapeDtypeStruct((t, d), F32),
        scratch_shapes=[pltpu.VMEM((2, TOP_K, tm, d), F32), pltpu.SemaphoreType.DMA((2,))],
        compiler_params=_params("arbitrary"),
        name="moe_combine",
    )(dest, dest, x1, gates, lw["sg"], lw["su"], lw["sd"], lw["ln2g"], lw["ln2b"], yb)


def _swap_halves(w):
    half = w.shape[-1] // 2
    return jnp.concatenate([w[..., half:], w[..., :half]], axis=-1)


def _s5_discretise(log_dt, a_re, a_im, b_re, b_im, c_re, c_im):
    n_g, n_p = a_re.shape
    n_h = b_re.shape[2]
    dt = jnp.exp(log_dt)[:, None]
    mag = jnp.exp(a_re * dt)
    ab_re, ab_im = mag * jnp.cos(a_im * dt), mag * jnp.sin(a_im * dt)
    den = jnp.square(a_re) + jnp.square(a_im)
    f_re = ((ab_re - 1.0) * a_re + ab_im * a_im) / den
    f_im = (ab_im * a_re - (ab_re - 1.0) * a_im) / den
    bb_re = f_re[..., None] * b_re - f_im[..., None] * b_im
    bb_im = f_re[..., None] * b_im + f_im[..., None] * b_re
    eye = jnp.eye(n_g, dtype=F32)
    blk = lambda m: jnp.einsum("gph,gk->ghkp", m, eye).reshape(n_g * n_h, n_g * n_p)
    wbu = jnp.concatenate([blk(bb_re), blk(bb_im)], axis=1)
    blk_c = lambda m: jnp.einsum("ghp,gk->kpgh", m, eye).reshape(n_g * n_p, n_g * n_h)
    wc = jnp.concatenate([blk_c(c_re), blk_c(-c_im)], axis=0)
    a = jnp.stack([ab_re.reshape(-1), ab_im.reshape(-1)], axis=0)
    return a, wbu.astype(BF16), wc.astype(BF16)


def _layer_weights(l, w, s_new):
    (w_in, q_norm_g, w_uq, kv_norm_g, w_uk, w_uv, ssm_log_dt, ssm_a_re, ssm_a_im, ssm_b_re, ssm_b_im, ssm_c_re,
     ssm_c_im, ssm_d, ssm_w_glu, ssm_b_glu, gmlp_ln_g, gmlp_ln_b, gmlp_w_s, gmlp_b_s, w_o, ln1_g, ln1_b, router_w,
     router_bias, sh_w_gate, sh_w_up, sh_w_down, ln2_g, ln2_b) = [a[l] for a in w]
    kv_rank, heads, nope = w_uk.shape
    q_rank = w_uq.shape[0]
    rope = w_uq.shape[1] // heads - nope
    v_dim = w_uv.shape[2]
    ssm_w = ssm_d.shape[0]
    g_heads, g_dim = gmlp_ln_g.shape
    gw = g_heads * g_dim
    chunk = gmlp_w_s.shape[1]
    o0, o1, o2, o3 = q_rank, q_rank + kv_rank, q_rank + kv_rank + rope, q_rank + kv_rank + rope + ssm_w
    w_kr = w_in[:, o1:o2]
    uq = w_uq.reshape(q_rank, heads, nope + rope)
    uq_r = uq[:, :, nope:]
    a, wbu, wc = _s5_discretise(ssm_log_dt, ssm_a_re, ssm_a_im, ssm_b_re, ssm_b_im, ssm_c_re, ssm_c_im)
    tril = jnp.tril(jnp.ones((chunk, chunk), bool))
    mix_p = jnp.where(tril, gmlp_w_s, 0.0)
    tril_s = jnp.tril(jnp.ones((s_new, s_new), bool))
    small = jnp.where(tril_s, gmlp_w_s[:, :s_new, :s_new], 0.0)
    reps = chunk // s_new
    mix_s = jnp.einsum("ab,hts->hatbs", jnp.eye(reps, dtype=F32), small).reshape(g_heads, chunk, chunk)
    bias_p = jnp.broadcast_to(gmlp_b_s.T[:, :, None], (chunk, g_heads, g_dim)).reshape(chunk, gw)
    bias_s = jnp.tile(jnp.broadcast_to(gmlp_b_s[:, :s_new].T[:, :, None], (s_new, g_heads, g_dim)).reshape(s_new, gw),
                      (reps, 1))
    attn_w = heads * v_dim
    n_e = router_w.shape[1]
    return {
        "wq": w_in[:, :o0].astype(BF16), "wkv": w_in[:, o0:o1].astype(BF16),
        "wkr": jnp.concatenate([w_kr, _swap_halves(w_kr)], axis=1).astype(BF16),
        "wu": w_in[:, o2:o3].astype(BF16), "wz": w_in[:, o3:].astype(BF16),
        "gq": q_norm_g[None, :], "gkv": kv_norm_g[None, :],
        "lng": gmlp_ln_g.reshape(1, gw), "lnb": gmlp_ln_b.reshape(1, gw),
        "mix": jnp.stack([mix_p, mix_s]).astype(BF16), "mixb": jnp.stack([bias_p, bias_s]),
        "wqn": jnp.transpose(uq[:, :, :nope], (1, 0, 2)).astype(BF16),
        "wqr": jnp.transpose(jnp.concatenate([uq_r, _swap_halves(uq_r)], axis=2), (1, 0, 2)).astype(BF16),
        "wuk": jnp.transpose(w_uk, (1, 2, 0)).astype(BF16),
        "wuv": jnp.transpose(w_uv, (1, 0, 2)).astype(BF16),
        "a": a, "wbu": wbu, "wc": wc, "d": ssm_d[None, :], "wglu": ssm_w_glu.astype(BF16), "bglu": ssm_b_glu[None, :],
        "woa": w_o[:attn_w].astype(BF16), "wos": w_o[attn_w:attn_w + ssm_w].astype(BF16),
        "wog": w_o[attn_w + ssm_w:].astype(BF16),
        "ln1g": ln1_g[None, :], "ln1b": ln1_b[None, :],
        "rwt": router_w.T, "rbias": jnp.broadcast_to(router_bias[:, None], (n_e, ROW_TILE)),
        "sg": sh_w_gate.astype(BF16), "su": sh_w_up.astype(BF16), "sd": sh_w_down.astype(BF16),
        "ln2g": ln2_g[None, :], "ln2b": ln2_b[None, :],
    }


def _rope_tables(pos, rope):
    half = rope // 2
    inv_freq = ROPE_THETA ** (-jnp.arange(half, dtype=F32) / half)
    ang = pos.astype(F32)[:, None] * inv_freq[None, :]
    cos, sin = jnp.cos(ang), jnp.sin(ang)
    return {"cos": jnp.concatenate([cos, cos], axis=1), "sin": jnp.concatenate([-sin, sin], axis=1)}


def _block_table(counts, n_rows):
    bm = MOE_ROWS
    padded = (counts + bm - 1) // bm * bm
    ends = jnp.cumsum(padded)
    n_blocks = ends[-1] // bm
    blocks = jnp.arange(n_rows // bm, dtype=I32)
    first_row = jnp.minimum(blocks, n_blocks - 1) * bm
    be = jnp.sum((ends[None, :] <= first_row[:, None]).astype(I32), axis=1)
    n_e = counts.shape[0]
    experts = jnp.arange(n_e, dtype=I32)
    later = (experts[None, :] > experts[:, None]) & (counts[None, :] > 0)
    next_e = jnp.min(jnp.where(later, experts[None, :], n_e), axis=1)
    next_e = jnp.where(next_e < n_e, next_e, -1).astype(I32)
    return ends - padded, jnp.minimum(be, n_e - 1), n_blocks.reshape(1).astype(I32), next_e


def _row_slots(idx, rank, offs):
    experts = jnp.arange(offs.shape[0], dtype=I32)
    return rank + jnp.sum(jnp.where(idx[..., None] == experts, offs, 0), axis=-1)


def kernel(x_prompt, x_sample, cache_kv, cache_kr, state_ssm_re, state_ssm_im, page_table, w_in, q_norm_g, w_uq, kv_norm_g, w_uk, w_uv, ssm_log_dt, ssm_a_re, ssm_a_im, ssm_b_re, ssm_b_im, ssm_c_re, ssm_c_im, ssm_d, ssm_w_glu, ssm_b_glu, gmlp_ln_g, gmlp_ln_b, gmlp_w_s, gmlp_b_s, w_o, ln1_g, ln1_b, router_w, router_bias, exp_w_gate, exp_w_up, exp_w_down, sh_w_gate, sh_w_up, sh_w_down, ln2_g, ln2_b):
    depth = w_in.shape[0]
    batch, seq, d = x_prompt.shape
    dec_b, s_new, _ = x_sample.shape
    n_pages, page = page_table.shape[1], cache_kv.shape[2]
    past = n_pages * page
    tp, ts = batch * seq, dec_b * s_new
    t = tp + ts
    kv_rank, heads, nope = w_uk.shape[1:]
    rope = cache_kr.shape[3]
    n_g, n_p = ssm_a_re.shape[1:]
    n_e = router_w.shape[2]
    assert tp % ROW_TILE == 0 and ts % ROW_TILE == 0 and seq % ROW_TILE == 0 and ROW_TILE % s_new == 0
    assert seq % KV_TILE == 0 and KV_TILE % Q_TILE == 0 and t % MOE_TOKENS == 0 and heads % Q_CHUNK_HEADS == 0
    assert n_pages % PAGE_GROUP == 0 and KEY_CHUNK % page == 0 and (PAGE_GROUP * page) % KEY_CHUNK == 0
    assert dec_b * (n_pages // PAGE_GROUP) >= PAGE_SLOTS
    assert kv_rank + rope <= QK_PAD and s_new <= LANES_V7X
    alpha = (2.0 * depth) ** 0.25
    qscale = (nope + rope) ** -0.5 * math.log2(math.e)
    n_rows = t * TOP_K + n_e * MOE_ROWS
    cache_krt = jnp.swapaxes(cache_kr, 2, 3)

    small = (w_in, q_norm_g, w_uq, kv_norm_g, w_uk, w_uv, ssm_log_dt, ssm_a_re, ssm_a_im, ssm_b_re, ssm_b_im,
             ssm_c_re, ssm_c_im, ssm_d, ssm_w_glu, ssm_b_glu, gmlp_ln_g, gmlp_ln_b, gmlp_w_s, gmlp_b_s, w_o, ln1_g,
             ln1_b, router_w, router_bias, sh_w_gate, sh_w_up, sh_w_down, ln2_g, ln2_b)
    pos = jnp.concatenate([jnp.tile(jnp.arange(seq, dtype=I32), batch),
                           jnp.tile(past + jnp.arange(s_new, dtype=I32), dec_b)])
    tabs = _rope_tables(pos, rope)
    x = jnp.concatenate([x_prompt.reshape(tp, d), x_sample.reshape(ts, d)], axis=0)
    h0_p = jnp.zeros((batch, 1, 2 * n_g * n_p), F32)
    per_layer = []
    for l in range(depth):
        lw = _layer_weights(l, small, s_new)
        cq, c, kr, kcat, u, gm, v = _in_proj(x, lw, tabs, tp // ROW_TILE)
        q_p = _mla_q(cq, lw, tabs, 0, tp, BF16, qscale)
        q_s = _mla_q(cq, lw, tabs, tp, ts, F32, qscale)
        attn_p = _attn_prompt(q_p, kcat, lw["wuv"], batch, seq)
        attn_s = _attn_sample(q_s, c, kr, tp, lw["wuv"], cache_kv, cache_krt, page_table, l)
        h0_s = jnp.concatenate([state_ssm_re[l].reshape(dec_b, 1, -1), state_ssm_im[l].reshape(dec_b, 1, -1)], axis=2)
        ssm_p, hT_p = _s5(u, h0_p, lw, 0, tp, seq)
        ssm_s, hT_s = _s5(u, h0_s, lw, tp, ts, s_new)
        x1, xp = _out_proj(x, attn_p, attn_s, ssm_p, ssm_s, gm, lw, alpha)
        idx, gates, rank, cnt = _router(x1, lw)
        offs, block_e, n_blocks, next_e = _block_table(cnt[:, 0].astype(I32), n_rows)
        dest = _row_slots(idx, rank, offs)
        xs = _dispatch(xp, dest, n_rows)
        yb = _experts(xs, block_e, n_blocks, next_e, exp_w_gate, exp_w_up, exp_w_down, l)
        x = _combine(x1, gates, dest, yb, lw, alpha)
        per_layer.append((c, kr, hT_p, hT_s, v))

    half = n_g * n_p
    st = lambda f: jnp.stack([f(p) for p in per_layer], axis=0)
    return (
        x[:tp].reshape(batch, seq, d),
        x[tp:].reshape(dec_b, s_new, d),
        st(lambda p: p[0][:tp].reshape(batch, seq, kv_rank)),
        st(lambda p: p[1][:tp].reshape(batch, seq, rope)),
        st(lambda p: p[2][:, 0, :half].reshape(batch, n_g, n_p)),
        st(lambda p: p[2][:, 0, half:].reshape(batch, n_g, n_p)),
        st(lambda p: p[0][tp:].reshape(dec_b, s_new, kv_rank)),
        st(lambda p: p[1][tp:].reshape(dec_b, s_new, rope)),
        st(lambda p: p[3][:, 0, :half].reshape(dec_b, n_g, n_p)),
        st(lambda p: p[3][:, 0, half:].reshape(dec_b, n_g, n_p)),
        st(lambda p: p[4][tp:].reshape(dec_b, s_new, -1)),
    )
```

```python
import functools
import math

import jax
import jax.numpy as jnp
from jax import lax
from jax.experimental import pallas as pl
from jax.experimental.pallas import tpu as pltpu

F32 = jnp.float32
BF16 = jnp.bfloat16
I32 = jnp.int32
U32 = jnp.uint32

LANES_V7X = 128
VMEM_LIMIT_V7X = 56 * 1024 * 1024

ROPE_THETA = 10000.0
EPS = 1e-6
TOP_K = 8
N_EXPERT_GROUPS = 8
TOPK_GROUPS = 4
ROUTE_SCALE = 2.5

ROW_TILE = 256
MOE_ROWS = 256
MOE_TOKENS = 128
Q_TILE = 128
KV_TILE = 256
Q_CHUNK_HEADS = 2
PAGE_GROUP = 16
PAGE_SLOTS = 4
KEY_CHUNK = 1024
QK_PAD = 384


def _params(*sem):
    return pltpu.CompilerParams(dimension_semantics=sem, vmem_limit_bytes=VMEM_LIMIT_V7X)


def _gelu(x):
    c = math.sqrt(2.0 / math.pi)
    return x * (0.5 * (1.0 + jnp.tanh(c * (x + 0.044715 * (x * x * x)))))


def _rms(x, g):
    return x * lax.rsqrt(jnp.mean(x * x, axis=-1, keepdims=True) + EPS) * g


def _layer_norm(x, g, b):
    mu = jnp.mean(x, axis=-1, keepdims=True)
    xc = x - mu
    var = jnp.mean(xc * xc, axis=-1, keepdims=True)
    return xc * lax.rsqrt(var + EPS) * g + b


def _dot(a, b):
    return jnp.dot(a, b, preferred_element_type=F32)


def _dot_nt(a, b):
    return lax.dot_general(a, b, (((1,), (1,)), ((), ())), preferred_element_type=F32)


def _in_proj_kernel(x_ref, wq_ref, wkv_ref, wkr_ref, wu_ref, wz_ref, gq_ref, gkv_ref, cos_ref, sin_ref,
                    lng_ref, lnb_ref, mix_ref, mixb_ref,
                    cq_ref, c_ref, kr_ref, kcat_ref, u_ref, gm_ref, v_ref, *, kv_rank, rope, g_heads, g_dim, chunk):
    xb = x_ref[...].astype(BF16)
    cq_ref[...] = _rms(_dot(xb, wq_ref[...]), gq_ref[...]).astype(BF16)
    c = _rms(_dot(xb, wkv_ref[...]), gkv_ref[...])
    c_ref[...] = c
    kr2 = _dot(xb, wkr_ref[...])
    kr = kr2[:, :rope] * cos_ref[...] + kr2[:, rope:] * sin_ref[...]
    kr_ref[...] = kr
    kcat_ref[:, :kv_rank] = c.astype(BF16)
    kcat_ref[:, kv_rank:kv_rank + rope] = kr.astype(BF16)
    kcat_ref[:, kv_rank + rope:] = jnp.zeros((kcat_ref.shape[0], kcat_ref.shape[1] - kv_rank - rope), BF16)
    u_ref[...] = _dot(xb, wu_ref[...])
    z = _gelu(_dot(xb, wz_ref[...]))
    width = g_heads * g_dim
    for h in range(g_heads):
        sl = slice(h * g_dim, (h + 1) * g_dim)
        vh = _layer_norm(z[:, width + h * g_dim: width + (h + 1) * g_dim], lng_ref[:, sl], lnb_ref[:, sl])
        v_ref[:, sl] = vh
        vb = vh.astype(BF16)
        for j in range(x_ref.shape[0] // chunk):
            rows = slice(j * chunk, (j + 1) * chunk)
            sv = _dot(mix_ref[h], vb[rows]) + mixb_ref[:, sl]
            gm_ref[rows, sl] = (z[rows, sl] * sv).astype(BF16)


def _in_proj(x, lw, tabs, n_prompt_tiles):
    t, d = x.shape
    tm = ROW_TILE
    kv_rank, rope = lw["wkv"].shape[1], lw["wkr"].shape[1] // 2
    gw = lw["lng"].shape[1]
    g_heads, chunk = lw["mix"].shape[1], lw["mix"].shape[2]
    row = lambda i: (i, 0)
    full = lambda i: (0, 0)
    trunk = lambda i: (jnp.where(i >= n_prompt_tiles, 1, 0), 0, 0, 0)
    trunk3 = lambda i: (jnp.where(i >= n_prompt_tiles, 1, 0), 0, 0)
    kern = functools.partial(_in_proj_kernel, kv_rank=kv_rank, rope=rope, g_heads=g_heads, g_dim=gw // g_heads,
                             chunk=chunk)
    q_rank, ssm_w = lw["wq"].shape[1], lw["wu"].shape[1]
    return pl.pallas_call(
        kern,
        grid=(t // tm,),
        in_specs=[
            pl.BlockSpec((tm, d), row),
            pl.BlockSpec(lw["wq"].shape, full), pl.BlockSpec(lw["wkv"].shape, full),
            pl.BlockSpec(lw["wkr"].shape, full), pl.BlockSpec(lw["wu"].shape, full),
            pl.BlockSpec(lw["wz"].shape, full),
            pl.BlockSpec(lw["gq"].shape, full), pl.BlockSpec(lw["gkv"].shape, full),
            pl.BlockSpec((tm, rope), row), pl.BlockSpec((tm, rope), row),
            pl.BlockSpec(lw["lng"].shape, full), pl.BlockSpec(lw["lnb"].shape, full),
            pl.BlockSpec((None,) + lw["mix"].shape[1:], trunk),
            pl.BlockSpec((None,) + lw["mixb"].shape[1:], trunk3),
        ],
        out_specs=[
            pl.BlockSpec((tm, q_rank), row), pl.BlockSpec((tm, kv_rank), row), pl.BlockSpec((tm, rope), row),
            pl.BlockSpec((tm, QK_PAD), row), pl.BlockSpec((tm, ssm_w), row), pl.BlockSpec((tm, gw), row),
            pl.BlockSpec((tm, gw), row),
        ],
        out_shape=[
            jax.ShapeDtypeStruct((t, q_rank), BF16), jax.ShapeDtypeStruct((t, kv_rank), F32),
            jax.ShapeDtypeStruct((t, rope), F32), jax.ShapeDtypeStruct((t, QK_PAD), BF16),
            jax.ShapeDtypeStruct((t, ssm_w), F32), jax.ShapeDtypeStruct((t, gw), BF16),
            jax.ShapeDtypeStruct((t, gw), F32),
        ],
        compiler_params=_params("parallel"),
        name="in_proj",
    )(x, lw["wq"], lw["wkv"], lw["wkr"], lw["wu"], lw["wz"], lw["gq"], lw["gkv"], tabs["cos"], tabs["sin"],
      lw["lng"], lw["lnb"], lw["mix"], lw["mixb"])


def _mla_q_kernel(cq_ref, wqn_ref, wqr_ref, wuk_ref, cos_ref, sin_ref, q_ref, *, kv_rank, rope, qscale):
    cq = cq_ref[...]
    for h in range(q_ref.shape[0]):
        qn = _dot(cq, wqn_ref[h]).astype(BF16)
        q_ref[h, :, :kv_rank] = (_dot(qn, wuk_ref[h]) * qscale).astype(q_ref.dtype)
        qr2 = _dot(cq, wqr_ref[h])
        qr = qr2[:, :rope] * cos_ref[...] + qr2[:, rope:] * sin_ref[...]
        q_ref[h, :, kv_rank:kv_rank + rope] = (qr * qscale).astype(q_ref.dtype)
        q_ref[h, :, kv_rank + rope:] = jnp.zeros((q_ref.shape[1], q_ref.shape[2] - kv_rank - rope), q_ref.dtype)


def _mla_q(cq, lw, tabs, row0, n_rows, out_dtype, qscale):
    tm = ROW_TILE
    heads, q_rank, _ = lw["wqn"].shape
    kv_rank, rope = lw["wuk"].shape[2], lw["wqr"].shape[2] // 2
    off = row0 // tm
    row = lambda i: (i + off, 0)
    full = lambda i: (0, 0, 0)
    return pl.pallas_call(
        functools.partial(_mla_q_kernel, kv_rank=kv_rank, rope=rope, qscale=qscale),
        grid=(n_rows // tm,),
        in_specs=[
            pl.BlockSpec((tm, q_rank), row),
            pl.BlockSpec(lw["wqn"].shape, full), pl.BlockSpec(lw["wqr"].shape, full),
            pl.BlockSpec(lw["wuk"].shape, full),
            pl.BlockSpec((tm, rope), row), pl.BlockSpec((tm, rope), row),
        ],
        out_specs=pl.BlockSpec((heads, tm, QK_PAD), lambda i: (0, i, 0)),
        out_shape=jax.ShapeDtypeStruct((heads, n_rows, QK_PAD), out_dtype),
        compiler_params=_params("parallel"),
        name="mla_q",
    )(cq, lw["wqn"], lw["wqr"], lw["wuk"], tabs["cos"], tabs["sin"])


def _lane_tiles(x, op):
    tiles = [x[:, i * LANES_V7X:(i + 1) * LANES_V7X] for i in range(x.shape[1] // LANES_V7X)]
    return functools.reduce(op, tiles)


def _widen(x, width):
    return jnp.tile(x, (1, width // LANES_V7X))


def _attn_prompt_kernel(q_ref, k_ref, wuv_ref, o_ref, s_scr, m_scr, l_scr, acc_scr, *, tq, tk, kv_rank, hpc):
    qi = pl.program_id(1)
    heads = q_ref.shape[0]
    cr = hpc * tq
    q = q_ref[...].reshape(heads * tq, q_ref.shape[2])
    m_scr[...] = jnp.full(m_scr.shape, -jnp.inf, F32)
    l_scr[...] = jnp.zeros(l_scr.shape, F32)
    acc_scr[...] = jnp.zeros(acc_scr.shape, F32)
    n_full = (qi * tq) // tk

    def keys(j):
        return k_ref[pl.ds(pl.multiple_of(j * tk, tk), tk), :]

    def scores(j, masked):
        s = _dot_nt(q, keys(j))
        if masked:
            qpos = qi * tq + lax.broadcasted_iota(I32, s.shape, 0) % tq
            kpos = j * tk + lax.broadcasted_iota(I32, s.shape, 1)
            s = jnp.where(kpos <= qpos, s, -jnp.inf)
        s_scr[j % 2] = s

    def weigh(j):
        v = keys(j)[:, :kv_rank]
        for c in range(heads // hpc):
            rows = slice(c * cr, (c + 1) * cr)
            s = s_scr[j % 2, rows]
            m_prev = m_scr[rows]
            m_new = jnp.maximum(m_prev, jnp.max(s, axis=-1, keepdims=True))
            alpha = jnp.exp2(m_prev - m_new)
            p = jnp.exp2(s - _widen(m_new, tk))
            l_scr[rows] = alpha * l_scr[rows] + _lane_tiles(p, jnp.add)
            acc_scr[rows] = _widen(alpha, kv_rank) * acc_scr[rows] + _dot(p.astype(BF16), v)
            m_scr[rows] = m_new

    @pl.when(n_full == 0)
    def _():
        scores(0, True)

    @pl.when(n_full > 0)
    def _():
        scores(0, False)

        def step(j, carry):
            weigh(j)
            scores(j + 1, False)
            return carry

        lax.fori_loop(0, n_full - 1, step, 0)
        weigh(n_full - 1)
        scores(n_full, True)

    weigh(n_full)
    o = (acc_scr[...] / jnp.sum(l_scr[...], axis=-1, keepdims=True)).astype(BF16)
    v_dim = wuv_ref.shape[2]
    for h in range(heads):
        o_ref[:, h * v_dim:(h + 1) * v_dim] = _dot(o[h * tq:(h + 1) * tq], wuv_ref[h])


def _attn_prompt(q, kcat, wuv, batch, seq):
    heads, tp, _ = q.shape
    kv_rank, v_dim = wuv.shape[1], wuv.shape[2]
    tq, tk = Q_TILE, KV_TILE
    nq = seq // tq
    rows = heads * tq
    return pl.pallas_call(
        functools.partial(_attn_prompt_kernel, tq=tq, tk=tk, kv_rank=kv_rank, hpc=Q_CHUNK_HEADS),
        grid=(batch, nq),
        in_specs=[
            pl.BlockSpec((heads, tq, QK_PAD), lambda b, i: (0, b * nq + i, 0)),
            pl.BlockSpec((seq, QK_PAD), lambda b, i: (b, 0)),
            pl.BlockSpec(wuv.shape, lambda b, i: (0, 0, 0)),
        ],
        out_specs=pl.BlockSpec((tq, heads * v_dim), lambda b, i: (b * nq + i, 0)),
        out_shape=jax.ShapeDtypeStruct((tp, heads * v_dim), F32),
        scratch_shapes=[pltpu.VMEM((2, rows, tk), F32), pltpu.VMEM((rows, LANES_V7X), F32),
                        pltpu.VMEM((rows, LANES_V7X), F32), pltpu.VMEM((rows, kv_rank), F32)],
        compiler_params=_params("parallel", "arbitrary"),
        name="attn_prompt",
    )(q, kcat, wuv)


def _attn_sample_kernel(pt_ref, q_ref, cnew_ref, krnew_ref, wuv_ref, ckv_hbm, ckrt_hbm, o_ref,
                        kvbuf, krbuf, kb_all, krt_scr, knew_scr, s_scr, sems,
                        *, layer, n_pages, group, page, kv_rank, rope, chunk):
    b = pl.program_id(0)
    heads, s_new, _ = q_ref.shape
    rows = heads * s_new
    n_groups = n_pages // group
    ppc = chunk // page
    cpg = group // ppc
    q = q_ref[...].reshape(rows, q_ref.shape[2]).astype(BF16)
    q_lat, q_rope = q[:, :kv_rank], q[:, kv_rank:]

    n_slots = kvbuf.shape[0]
    ahead = n_slots - 1
    total = pl.num_programs(0) * n_groups

    def copies(gg, slot, p):
        pg = pt_ref[gg * group + p]
        return (pltpu.make_async_copy(ckv_hbm.at[layer, pg], kvbuf.at[slot, p], sems.at[0, slot]),
                pltpu.make_async_copy(ckrt_hbm.at[layer, pg], krbuf.at[slot, p], sems.at[1, slot]))

    def start(gg):
        for p in range(group):
            for cp in copies(gg, gg % n_slots, p):
                cp.start()

    def wait(slot):
        for p in range(group):
            for cp in copies(0, slot, p):
                cp.wait()

    @pl.when(b == 0)
    def _():
        for gg in range(ahead):
            start(gg)

    krt_scr[:, rope:, :] = jnp.zeros((cpg, krt_scr.shape[1] - rope, chunk), BF16)

    def stream(g, m_part):
        gg = b * n_groups + g
        slot = gg % n_slots

        @pl.when(gg + ahead < total)
        def _():
            start(gg + ahead)

        wait(slot)
        for c in range(cpg):
            kvb = kvbuf[slot, c * ppc:(c + 1) * ppc].reshape(chunk, kv_rank).astype(BF16)
            kb_all[g * cpg + c] = kvb
            for p in range(ppc):
                krt_scr[c, :rope, p * page:(p + 1) * page] = krbuf[slot, c * ppc + p].astype(BF16)
            s = _dot_nt(q_lat, kvb) + _dot(q_rope, krt_scr[c])
            s_scr[g * cpg + c] = s
            m_part = jnp.maximum(m_part, _lane_tiles(s, jnp.maximum))
        return m_part

    m_part = lax.fori_loop(0, n_groups, stream, jnp.full((rows, LANES_V7X), -jnp.inf, F32))

    knew_scr[...] = jnp.zeros(knew_scr.shape, BF16)
    knew_scr[:s_new, :kv_rank] = cnew_ref[...].astype(BF16)
    knew_scr[:s_new, kv_rank:kv_rank + rope] = krnew_ref[...].astype(BF16)
    kn = knew_scr[...]
    sn = _dot_nt(q, kn)
    qpos = lax.broadcasted_iota(I32, sn.shape, 0) % s_new
    kpos = lax.broadcasted_iota(I32, sn.shape, 1)
    sn = jnp.where(kpos <= qpos, sn, -jnp.inf)
    m = jnp.max(jnp.maximum(m_part, sn), axis=-1, keepdims=True)
    m_wide = jnp.broadcast_to(m, (rows, chunk))

    def weigh(i, carry):
        l_part, acc = carry
        p = jnp.exp2(s_scr[i] - m_wide)
        return l_part + _lane_tiles(p, jnp.add), acc + _dot(p.astype(BF16), kb_all[i])

    pn = jnp.exp2(sn - m_wide[:, :LANES_V7X])
    l_part, acc = lax.fori_loop(0, n_groups * cpg, weigh, (pn, _dot(pn.astype(BF16), kn[:, :kv_rank])), unroll=2)
    o = (acc / jnp.sum(l_part, axis=-1, keepdims=True)).astype(BF16)
    v_dim = wuv_ref.shape[2]
    for h in range(heads):
        o_ref[:, h * v_dim:(h + 1) * v_dim] = _dot(o, wuv_ref[h])[h * s_new:(h + 1) * s_new]


def _attn_sample(q, c, kr, row0, wuv, cache_kv, cache_krt, page_table, layer):
    heads, ts, _ = q.shape
    dec_b, n_pages = page_table.shape
    s_new = ts // dec_b
    page, kv_rank = cache_kv.shape[2], cache_kv.shape[3]
    rope = cache_krt.shape[2]
    v_dim = wuv.shape[2]
    group, chunk = PAGE_GROUP, KEY_CHUNK
    n_chunks = n_pages * page // chunk
    rows = heads * s_new
    off = row0 // s_new
    grid_spec = pltpu.PrefetchScalarGridSpec(
        num_scalar_prefetch=1,
        grid=(dec_b,),
        in_specs=[
            pl.BlockSpec((heads, s_new, QK_PAD), lambda b, pt: (0, b, 0)),
            pl.BlockSpec((s_new, kv_rank), lambda b, pt: (b + off, 0)),
            pl.BlockSpec((s_new, rope), lambda b, pt: (b + off, 0)),
            pl.BlockSpec(wuv.shape, lambda b, pt: (0, 0, 0)),
            pl.BlockSpec(memory_space=pl.ANY),
            pl.BlockSpec(memory_space=pl.ANY),
        ],
        out_specs=pl.BlockSpec((s_new, heads * v_dim), lambda b, pt: (b, 0)),
        scratch_shapes=[
            pltpu.VMEM((PAGE_SLOTS, group, page, kv_rank), F32),
            pltpu.VMEM((PAGE_SLOTS, group, rope, page), F32),
            pltpu.VMEM((n_chunks, chunk, kv_rank), BF16),
            pltpu.VMEM((group * page // chunk, QK_PAD - kv_rank, chunk), BF16),
            pltpu.VMEM((LANES_V7X, QK_PAD), BF16),
            pltpu.VMEM((n_chunks, rows, chunk), F32),
            pltpu.SemaphoreType.DMA((2, PAGE_SLOTS)),
        ],
    )
    return pl.pallas_call(
        functools.partial(_attn_sample_kernel, layer=layer, n_pages=n_pages, group=group, page=page,
                          kv_rank=kv_rank, rope=rope, chunk=chunk),
        grid_spec=grid_spec,
        out_shape=jax.ShapeDtypeStruct((ts, heads * v_dim), F32),
        compiler_params=_params("arbitrary"),
        name="attn_sample",
    )(page_table.reshape(-1), q, c, kr, wuv, cache_kv, cache_krt)


def _s5_kernel(u_ref, h0_ref, wbu_ref, a_ref, wc_ref, d_ref, wglu_ref, bglu_ref, y_ref, ht_ref, hs_scr, carry_scr,
               *, seq_rows, seqs_per_tile, tiles_per_seq):
    i = pl.program_id(0)
    u = u_ref[...]
    hs_scr[...] = _dot(u.astype(BF16), wbu_ref[...])
    n = a_ref.shape[1]
    a_re, a_im = a_ref[0:1, :], a_ref[1:2, :]

    if tiles_per_seq > 1:
        @pl.when(i % tiles_per_seq == 0)
        def _():
            carry_scr[...] = h0_ref[0]

    def one_sequence(s, carry):
        h = carry_scr[...] if tiles_per_seq > 1 else h0_ref[s]

        def row(t, hc):
            hr, hi = hc
            r = s * seq_rows + t
            nr = a_re * hr - a_im * hi + hs_scr[pl.ds(r, 1), :n]
            ni = a_re * hi + a_im * hr + hs_scr[pl.ds(r, 1), n:]
            hs_scr[pl.ds(r, 1), :n] = nr
            hs_scr[pl.ds(r, 1), n:] = ni
            return nr, ni

        hr, hi = lax.fori_loop(0, seq_rows, row, (h[:, :n], h[:, n:]), unroll=8)
        last = jnp.concatenate([hr, hi], axis=1)
        ht_ref[s] = last
        if tiles_per_seq > 1:
            carry_scr[...] = last
        return carry

    lax.fori_loop(0, seqs_per_tile, one_sequence, 0)
    y = _dot(hs_scr[...].astype(BF16), wc_ref[...]) + d_ref[...] * u
    g = _gelu(y)
    y_ref[...] = g * jax.nn.sigmoid(_dot(g.astype(BF16), wglu_ref[...]) + bglu_ref[...])


def _s5(u, h0, lw, row0, n_rows, seq_len):
    tc = ROW_TILE
    width = u.shape[1]
    n2 = lw["wbu"].shape[1]
    seq_rows = min(seq_len, tc)
    seqs_per_tile = tc // seq_rows
    tiles_per_seq = max(1, seq_len // tc)
    n_seq = n_rows // seq_len
    off = row0 // tc
    full = lambda i: (0, 0)
    seq_idx = (lambda i: (i // tiles_per_seq, 0, 0)) if tiles_per_seq > 1 else (lambda i: (i, 0, 0))
    return pl.pallas_call(
        functools.partial(_s5_kernel, seq_rows=seq_rows, seqs_per_tile=seqs_per_tile, tiles_per_seq=tiles_per_seq),
        grid=(n_rows // tc,),
        in_specs=[
            pl.BlockSpec((tc, width), lambda i: (i + off, 0)),
            pl.BlockSpec((seqs_per_tile, 1, n2), seq_idx),
            pl.BlockSpec(lw["wbu"].shape, full), pl.BlockSpec(lw["a"].shape, full),
            pl.BlockSpec(lw["wc"].shape, full), pl.BlockSpec(lw["d"].shape, full),
            pl.BlockSpec(lw["wglu"].shape, full), pl.BlockSpec(lw["bglu"].shape, full),
        ],
        out_specs=[pl.BlockSpec((tc, width), lambda i: (i, 0)), pl.BlockSpec((seqs_per_tile, 1, n2), seq_idx)],
        out_shape=[jax.ShapeDtypeStruct((n_rows, width), F32), jax.ShapeDtypeStruct((n_seq, 1, n2), F32)],
        scratch_shapes=[pltpu.VMEM((tc, n2), F32), pltpu.VMEM((1, n2), F32)],
        compiler_params=_params("arbitrary"),
        name="s5",
    )(u, h0, lw["wbu"], lw["a"], lw["wc"], lw["d"], lw["wglu"], lw["bglu"])


def _pack_bf16_pairs(x):
    half = x.shape[1] // 2
    hi = pltpu.bitcast(x[:, :half].astype(BF16).astype(F32), U32)
    lo = pltpu.bitcast(x[:, half:].astype(BF16).astype(F32), U32)
    return hi | (lo >> 16)


def _unpack_bf16_pairs(p):
    hi = pltpu.bitcast(p & jnp.uint32(0xFFFF0000), F32)
    lo = pltpu.bitcast(p << 16, F32)
    return hi, lo


def _store_row_tiles(ref, x):
    rows, n = x.shape[0], x.shape[1] // LANES_V7X
    for j in range(n):
        ref[pl.ds(j, rows, stride=n), :] = x[:, j * LANES_V7X:(j + 1) * LANES_V7X]


def _load_row_tiles(ref, rows):
    n = ref.shape[0] // rows
    return jnp.concatenate([ref[pl.ds(j, rows, stride=n), :] for j in range(n)], axis=1)


def _out_proj_kernel(x_ref, ap_ref, as_ref, sp_ref, ss_ref, g_ref, woa_ref, wos_ref, wog_ref, lg_ref, lb_ref,
                     x1_ref, xp_ref, *, alpha, n_prompt_tiles):
    prompt = pl.program_id(0) < n_prompt_tiles
    a = jnp.where(prompt, ap_ref[...], as_ref[...])
    s = jnp.where(prompt, sp_ref[...], ss_ref[...])
    mix = _dot(a.astype(BF16), woa_ref[...]) + _dot(s.astype(BF16), wos_ref[...]) + _dot(g_ref[...], wog_ref[...])
    x1 = _layer_norm(alpha * x_ref[...] + mix, lg_ref[...], lb_ref[...])
    x1_ref[...] = x1
    _store_row_tiles(xp_ref, _pack_bf16_pairs(x1))


def _out_proj(x, attn_p, attn_s, ssm_p, ssm_s, gm, lw, alpha):
    t, d = x.shape
    tm = ROW_TILE
    n_p, n_s = attn_p.shape[0] // tm, attn_s.shape[0] // tm
    row_sub = d // 2 // LANES_V7X
    row = lambda i: (i, 0)
    full = lambda i: (0, 0)
    row_p = lambda i: (jnp.minimum(i, n_p - 1), 0)
    row_s = lambda i: (jnp.clip(i - n_p, 0, n_s - 1), 0)
    return pl.pallas_call(
        functools.partial(_out_proj_kernel, alpha=alpha, n_prompt_tiles=n_p),
        grid=(t // tm,),
        in_specs=[
            pl.BlockSpec((tm, d), row),
            pl.BlockSpec((tm, attn_p.shape[1]), row_p), pl.BlockSpec((tm, attn_s.shape[1]), row_s),
            pl.BlockSpec((tm, ssm_p.shape[1]), row_p), pl.BlockSpec((tm, ssm_s.shape[1]), row_s),
            pl.BlockSpec((tm, gm.shape[1]), row),
            pl.BlockSpec(lw["woa"].shape, full), pl.BlockSpec(lw["wos"].shape, full), pl.BlockSpec(lw["wog"].shape, full),
            pl.BlockSpec(lw["ln1g"].shape, full), pl.BlockSpec(lw["ln1b"].shape, full),
        ],
        out_specs=[pl.BlockSpec((tm, d), row), pl.BlockSpec((tm * row_sub, LANES_V7X), row)],
        out_shape=[jax.ShapeDtypeStruct((t, d), F32), jax.ShapeDtypeStruct((t * row_sub, LANES_V7X), U32)],
        compiler_params=_params("parallel"),
        name="out_proj",
    )(x, attn_p, attn_s, ssm_p, ssm_s, gm, lw["woa"], lw["wos"], lw["wog"], lw["ln1g"], lw["ln1b"])


def _first_index_of_max(x, iota, n):
    m = jnp.max(x, axis=0, keepdims=True)
    return m, jnp.min(jnp.where(x == m, iota, n), axis=0, keepdims=True)


def _router_kernel(x_ref, rwt_ref, bias_ref, idx_ref, gate_ref, rank_ref, cnt_ref, carry_scr):
    i = pl.program_id(0)

    @pl.when(i == 0)
    def _():
        carry_scr[...] = jnp.zeros(carry_scr.shape, F32)

    n_e = rwt_ref.shape[0]
    tm = x_ref.shape[0]
    per_group = n_e // N_EXPERT_GROUPS
    logits = lax.dot_general(rwt_ref[...], x_ref[...], (((1,), (1,)), ((), ())),
                             precision=lax.Precision.HIGHEST, preferred_element_type=F32)
    scores = jax.nn.sigmoid(logits)
    sel = scores + bias_ref[...]
    neg = -jnp.inf
    sub = lax.broadcasted_iota(I32, (per_group, tm), 0)
    group_scores = []
    for g in range(N_EXPERT_GROUPS):
        blk = sel[g * per_group:(g + 1) * per_group, :]
        m1, i1 = _first_index_of_max(blk, sub, per_group)
        m2 = jnp.max(jnp.where(sub == i1, neg, blk), axis=0, keepdims=True)
        group_scores.append(m1 + m2)
    work = jnp.concatenate(group_scores, axis=0)
    g_iota = lax.broadcasted_iota(I32, work.shape, 0)
    keep = jnp.zeros(work.shape, F32)
    for _ in range(TOPK_GROUPS):
        _, gi = _first_index_of_max(work, g_iota, N_EXPERT_GROUPS)
        hit = g_iota == gi
        keep = jnp.where(hit, 1.0, keep)
        work = jnp.where(hit, neg, work)
    keep_e = jnp.concatenate([jnp.broadcast_to(keep[g:g + 1, :], (per_group, tm)) for g in range(N_EXPERT_GROUPS)], axis=0)
    cand = jnp.where(keep_e > 0.0, sel, neg)
    e_iota = lax.broadcasted_iota(I32, (n_e, tm), 0)
    onehot = jnp.zeros((n_e, tm), F32)
    idx_rows, gate_rows = [], []
    for _ in range(TOP_K):
        _, ik = _first_index_of_max(cand, e_iota, n_e)
        hit = e_iota == ik
        gate_rows.append(jnp.sum(jnp.where(hit, scores, 0.0), axis=0, keepdims=True))
        onehot = jnp.where(hit, 1.0, onehot)
        cand = jnp.where(hit, neg, cand)
        idx_rows.append(ik)
    gates = jnp.concatenate(gate_rows, axis=0)
    gate_ref[...] = gates / jnp.sum(gates, axis=0, keepdims=True) * ROUTE_SCALE
    idx_ref[...] = jnp.concatenate(idx_rows, axis=0)
    before = (lax.broadcasted_iota(I32, (tm, tm), 0) < lax.broadcasted_iota(I32, (tm, tm), 1)).astype(BF16)
    prefix = _dot(onehot.astype(BF16), before) + carry_scr[:, :1]
    rank_rows = [jnp.sum(jnp.where(e_iota == ik, prefix, 0.0), axis=0, keepdims=True) for ik in idx_rows]
    rank_ref[...] = jnp.concatenate(rank_rows, axis=0).astype(I32)
    carry_scr[...] = carry_scr[...] + jnp.sum(onehot, axis=1, keepdims=True)
    cnt_ref[...] = carry_scr[...]


def _router(x1, lw):
    t, d = x1.shape
    tm = ROW_TILE
    n_e = lw["rwt"].shape[0]
    col = lambda i: (0, i)
    full = lambda i: (0, 0)
    return pl.pallas_call(
        _router_kernel,
        grid=(t // tm,),
        in_specs=[pl.BlockSpec((tm, d), lambda i: (i, 0)), pl.BlockSpec(lw["rwt"].shape, full),
                  pl.BlockSpec(lw["rbias"].shape, full)],
        out_specs=[pl.BlockSpec((TOP_K, tm), col), pl.BlockSpec((TOP_K, tm), col), pl.BlockSpec((TOP_K, tm), col),
                   pl.BlockSpec((n_e, LANES_V7X), full)],
        out_shape=[jax.ShapeDtypeStruct((TOP_K, t), I32), jax.ShapeDtypeStruct((TOP_K, t), F32),
                   jax.ShapeDtypeStruct((TOP_K, t), I32), jax.ShapeDtypeStruct((n_e, LANES_V7X), F32)],
        scratch_shapes=[pltpu.VMEM((n_e, LANES_V7X), F32)],
        compiler_params=_params("arbitrary"),
        name="router",
    )(x1, lw["rwt"], lw["rbias"])


def _dispatch_kernel(dest_ref, xp_ref, xs_in, xs_hbm, sem, *, row_sub):
    del xs_in
    tokens = xp_ref.shape[0] // row_sub

    def copy(tok, k):
        src = pl.ds(pl.multiple_of(tok * row_sub, row_sub), row_sub)
        dst = pl.ds(pl.multiple_of(dest_ref[k, tok] * row_sub, row_sub), row_sub)
        return pltpu.make_async_copy(xp_ref.at[src, :], xs_hbm.at[dst, :], sem)

    def start(tok, carry):
        for k in range(TOP_K):
            copy(tok, k).start()
        return carry

    def wait(tok, carry):
        for k in range(TOP_K):
            copy(tok, k).wait()
        return carry

    lax.fori_loop(0, tokens, start, 0)
    lax.fori_loop(0, tokens, wait, 0)


def _dispatch(xp, dest, xs_buf, row_sub):
    tm = MOE_TOKENS
    t = xp.shape[0] // row_sub
    return pl.pallas_call(
        functools.partial(_dispatch_kernel, row_sub=row_sub),
        grid=(t // tm,),
        in_specs=[pl.BlockSpec((TOP_K, tm), lambda i: (0, i), memory_space=pltpu.SMEM),
                  pl.BlockSpec((tm * row_sub, LANES_V7X), lambda i: (i, 0)),
                  pl.BlockSpec(memory_space=pl.ANY)],
        out_specs=pl.BlockSpec(memory_space=pl.ANY),
        out_shape=jax.ShapeDtypeStruct(xs_buf.shape, U32),
        scratch_shapes=[pltpu.SemaphoreType.DMA(())],
        input_output_aliases={2: 0},
        compiler_params=_params("arbitrary"),
        name="moe_dispatch",
    )(dest, xp, xs_buf)


def _experts_kernel(be_ref, nb_ref, nxt_ref, xs_ref, wg_hbm, wu_hbm, wd_hbm, y_ref, wg_f, wu_f, wd_f, wg_b, wu_b, wd_b,
                    sems, *, layer, rows):
    i = pl.program_id(0)
    e = be_ref[i]
    fresh = jnp.logical_or(i == 0, e != be_ref[jnp.maximum(i - 1, 0)])

    def copies(ex):
        return (pltpu.make_async_copy(wg_hbm.at[layer, ex], wg_f, sems.at[0]),
                pltpu.make_async_copy(wu_hbm.at[layer, ex], wu_f, sems.at[1]),
                pltpu.make_async_copy(wd_hbm.at[layer, ex], wd_f, sems.at[2]))

    @pl.when(i == 0)
    def _():
        for cp in copies(e):
            cp.start()

    @pl.when(fresh)
    def _():
        for cp in copies(e):
            cp.wait()
        wg_b[...] = wg_f[...].astype(BF16)
        wu_b[...] = wu_f[...].astype(BF16)
        wd_b[...] = wd_f[...].astype(BF16)

        @pl.when(nxt_ref[e] >= 0)
        def _():
            for cp in copies(nxt_ref[e]):
                cp.start()

    @pl.when(i < nb_ref[0])
    def _():
        hi, lo = _unpack_bf16_pairs(_load_row_tiles(xs_ref, rows))
        hi, lo = hi.astype(BF16), lo.astype(BF16)
        half = hi.shape[1]
        g = _dot(hi, wg_b[:half, :]) + _dot(lo, wg_b[half:, :])
        u = _dot(hi, wu_b[:half, :]) + _dot(lo, wu_b[half:, :])
        y = _dot((jax.nn.silu(g) * u).astype(BF16), wd_b[...])
        _store_row_tiles(y_ref, _pack_bf16_pairs(y))

    @pl.when(i >= nb_ref[0])
    def _():
        y_ref[...] = jnp.zeros(y_ref.shape, U32)


def _experts(xs, block_e, n_blocks, next_e, wg, wu, wd, layer, row_sub):
    bm = MOE_ROWS
    n_rows = xs.shape[0] // row_sub
    d, ff = wg.shape[2], wg.shape[3]
    blk = (bm * row_sub, LANES_V7X)
    grid_spec = pltpu.PrefetchScalarGridSpec(
        num_scalar_prefetch=3,
        grid=(n_rows // bm,),
        in_specs=[
            pl.BlockSpec(blk, lambda i, be, nb, nx: (jnp.minimum(i, nb[0] - 1), 0)),
            pl.BlockSpec(memory_space=pl.ANY), pl.BlockSpec(memory_space=pl.ANY), pl.BlockSpec(memory_space=pl.ANY),
        ],
        out_specs=pl.BlockSpec(blk, lambda i, be, nb, nx: (i, 0)),
        scratch_shapes=[pltpu.VMEM((d, ff), F32), pltpu.VMEM((d, ff), F32), pltpu.VMEM((ff, d), F32),
                        pltpu.VMEM((d, ff), BF16), pltpu.VMEM((d, ff), BF16), pltpu.VMEM((ff, d), BF16),
                        pltpu.SemaphoreType.DMA((3,))],
    )
    return pl.pallas_call(
        functools.partial(_experts_kernel, layer=layer, rows=bm),
        grid_spec=grid_spec,
        out_shape=jax.ShapeDtypeStruct(xs.shape, U32),
        compiler_params=_params("arbitrary"),
        name="moe_experts",
    )(block_e, n_blocks, next_e, xs, wg, wu, wd)


def _combine_kernel(dest_ref, dnext_ref, x1_ref, gate_ref, sg_ref, su_ref, sd_ref, lg_ref, lb_ref, yb_hbm, x2_ref,
                    buf, sems, *, alpha, row_sub):
    i = pl.program_id(0)
    tokens = x1_ref.shape[0]
    slot = i % 2

    def copy(dref, s, tok, k):
        src = pl.ds(pl.multiple_of(dref[k, tok] * row_sub, row_sub), row_sub)
        dst = pl.ds(pl.multiple_of(tok * row_sub, row_sub), row_sub)
        return pltpu.make_async_copy(yb_hbm.at[src, :], buf.at[s, k, dst, :], sems.at[s])

    def gather(dref, s):
        def start(tok, carry):
            for k in range(TOP_K):
                copy(dref, s, tok, k).start()
            return carry

        lax.fori_loop(0, tokens, start, 0)

    def wait(tok, carry):
        for k in range(TOP_K):
            copy(dest_ref, slot, tok, k).wait()
        return carry

    @pl.when(i == 0)
    def _():
        gather(dest_ref, 0)

    @pl.when(i + 1 < pl.num_programs(0))
    def _():
        gather(dnext_ref, 1 - slot)

    x1 = x1_ref[...]
    xb = x1.astype(BF16)
    h = (jax.nn.silu(_dot(xb, sg_ref[...])) * _dot(xb, su_ref[...])).astype(BF16)
    acc = alpha * x1 + _dot(h, sd_ref[...])
    gates = jnp.concatenate([gate_ref[...], jnp.zeros((tokens - TOP_K, tokens), F32)], axis=0).T
    lax.fori_loop(0, tokens, wait, 0)
    half = x1.shape[1] // 2
    routed_hi = jnp.zeros((tokens, half), F32)
    routed_lo = jnp.zeros((tokens, half), F32)
    for k in range(TOP_K):
        hi, lo = _unpack_bf16_pairs(_load_row_tiles(buf.at[slot, k], tokens))
        routed_hi = routed_hi + gates[:, k:k + 1] * hi
        routed_lo = routed_lo + gates[:, k:k + 1] * lo
    acc = acc + jnp.concatenate([routed_hi, routed_lo], axis=1)
    x2_ref[...] = _layer_norm(acc, lg_ref[...], lb_ref[...])


def _combine(x1, gates, dest, yb, lw, alpha, row_sub):
    t, d = x1.shape
    tm = MOE_TOKENS
    n = t // tm
    full = lambda i: (0, 0)
    return pl.pallas_call(
        functools.partial(_combine_kernel, alpha=alpha, row_sub=row_sub),
        grid=(n,),
        in_specs=[
            pl.BlockSpec((TOP_K, tm), lambda i: (0, i), memory_space=pltpu.SMEM),
            pl.BlockSpec((TOP_K, tm), lambda i: (0, jnp.minimum(i + 1, n - 1)), memory_space=pltpu.SMEM),
            pl.BlockSpec((tm, d), lambda i: (i, 0)),
            pl.BlockSpec((TOP_K, tm), lambda i: (0, i)),
            pl.BlockSpec(lw["sg"].shape, full), pl.BlockSpec(lw["su"].shape, full), pl.BlockSpec(lw["sd"].shape, full),
            pl.BlockSpec(lw["ln2g"].shape, full), pl.BlockSpec(lw["ln2b"].shape, full),
            pl.BlockSpec(memory_space=pl.ANY),
        ],
        out_specs=pl.BlockSpec((tm, d), lambda i: (i, 0)),
        out_shape=jax.ShapeDtypeStruct((t, d), F32),
        scratch_shapes=[pltpu.VMEM((2, TOP_K, tm * row_sub, LANES_V7X), U32), pltpu.SemaphoreType.DMA((2,))],
        compiler_params=_params("arbitrary"),
        name="moe_combine",
    )(dest, dest, x1, gates, lw["sg"], lw["su"], lw["sd"], lw["ln2g"], lw["ln2b"], yb)


def _swap_halves(w):
    half = w.shape[-1] // 2
    return jnp.concatenate([w[..., half:], w[..., :half]], axis=-1)


def _s5_discretise(log_dt, a_re, a_im, b_re, b_im, c_re, c_im):
    n_g, n_p = a_re.shape
    n_h = b_re.shape[2]
    dt = jnp.exp(log_dt)[:, None]
    mag = jnp.exp(a_re * dt)
    ab_re, ab_im = mag * jnp.cos(a_im * dt), mag * jnp.sin(a_im * dt)
    den = jnp.square(a_re) + jnp.square(a_im)
    f_re = ((ab_re - 1.0) * a_re + ab_im * a_im) / den
    f_im = (ab_im * a_re - (ab_re - 1.0) * a_im) / den
    bb_re = f_re[..., None] * b_re - f_im[..., None] * b_im
    bb_im = f_re[..., None] * b_im + f_im[..., None] * b_re
    eye = jnp.eye(n_g, dtype=F32)
    blk = lambda m: jnp.einsum("gph,gk->ghkp", m, eye).reshape(n_g * n_h, n_g * n_p)
    wbu = jnp.concatenate([blk(bb_re), blk(bb_im)], axis=1)
    blk_c = lambda m: jnp.einsum("ghp,gk->kpgh", m, eye).reshape(n_g * n_p, n_g * n_h)
    wc = jnp.concatenate([blk_c(c_re), blk_c(-c_im)], axis=0)
    a = jnp.stack([ab_re.reshape(-1), ab_im.reshape(-1)], axis=0)
    return a, wbu.astype(BF16), wc.astype(BF16)


def _layer_weights(l, w, s_new):
    (w_in, q_norm_g, w_uq, kv_norm_g, w_uk, w_uv, ssm_log_dt, ssm_a_re, ssm_a_im, ssm_b_re, ssm_b_im, ssm_c_re,
     ssm_c_im, ssm_d, ssm_w_glu, ssm_b_glu, gmlp_ln_g, gmlp_ln_b, gmlp_w_s, gmlp_b_s, w_o, ln1_g, ln1_b, router_w,
     router_bias, sh_w_gate, sh_w_up, sh_w_down, ln2_g, ln2_b) = [a[l] for a in w]
    kv_rank, heads, nope = w_uk.shape
    q_rank = w_uq.shape[0]
    rope = w_uq.shape[1] // heads - nope
    v_dim = w_uv.shape[2]
    ssm_w = ssm_d.shape[0]
    g_heads, g_dim = gmlp_ln_g.shape
    gw = g_heads * g_dim
    chunk = gmlp_w_s.shape[1]
    o0, o1, o2, o3 = q_rank, q_rank + kv_rank, q_rank + kv_rank + rope, q_rank + kv_rank + rope + ssm_w
    w_kr = w_in[:, o1:o2]
    uq = w_uq.reshape(q_rank, heads, nope + rope)
    uq_r = uq[:, :, nope:]
    a, wbu, wc = _s5_discretise(ssm_log_dt, ssm_a_re, ssm_a_im, ssm_b_re, ssm_b_im, ssm_c_re, ssm_c_im)
    tril = jnp.tril(jnp.ones((chunk, chunk), bool))
    mix_p = jnp.where(tril, gmlp_w_s, 0.0)
    tril_s = jnp.tril(jnp.ones((s_new, s_new), bool))
    small = jnp.where(tril_s, gmlp_w_s[:, :s_new, :s_new], 0.0)
    reps = chunk // s_new
    mix_s = jnp.einsum("ab,hts->hatbs", jnp.eye(reps, dtype=F32), small).reshape(g_heads, chunk, chunk)
    bias_p = jnp.broadcast_to(gmlp_b_s.T[:, :, None], (chunk, g_heads, g_dim)).reshape(chunk, gw)
    bias_s = jnp.tile(jnp.broadcast_to(gmlp_b_s[:, :s_new].T[:, :, None], (s_new, g_heads, g_dim)).reshape(s_new, gw),
                      (reps, 1))
    attn_w = heads * v_dim
    n_e = router_w.shape[1]
    return {
        "wq": w_in[:, :o0].astype(BF16), "wkv": w_in[:, o0:o1].astype(BF16),
        "wkr": jnp.concatenate([w_kr, _swap_halves(w_kr)], axis=1).astype(BF16),
        "wu": w_in[:, o2:o3].astype(BF16), "wz": w_in[:, o3:].astype(BF16),
        "gq": q_norm_g[None, :], "gkv": kv_norm_g[None, :],
        "lng": gmlp_ln_g.reshape(1, gw), "lnb": gmlp_ln_b.reshape(1, gw),
        "mix": jnp.stack([mix_p, mix_s]).astype(BF16), "mixb": jnp.stack([bias_p, bias_s]),
        "wqn": jnp.transpose(uq[:, :, :nope], (1, 0, 2)).astype(BF16),
        "wqr": jnp.transpose(jnp.concatenate([uq_r, _swap_halves(uq_r)], axis=2), (1, 0, 2)).astype(BF16),
        "wuk": jnp.transpose(w_uk, (1, 2, 0)).astype(BF16),
        "wuv": jnp.transpose(w_uv, (1, 0, 2)).astype(BF16),
        "a": a, "wbu": wbu, "wc": wc, "d": ssm_d[None, :], "wglu": ssm_w_glu.astype(BF16), "bglu": ssm_b_glu[None, :],
        "woa": w_o[:attn_w].astype(BF16), "wos": w_o[attn_w:attn_w + ssm_w].astype(BF16),
        "wog": w_o[attn_w + ssm_w:].astype(BF16),
        "ln1g": ln1_g[None, :], "ln1b": ln1_b[None, :],
        "rwt": router_w.T, "rbias": jnp.broadcast_to(router_bias[:, None], (n_e, ROW_TILE)),
        "sg": sh_w_gate.astype(BF16), "su": sh_w_up.astype(BF16), "sd": sh_w_down.astype(BF16),
        "ln2g": ln2_g[None, :], "ln2b": ln2_b[None, :],
    }


def _rope_tables(pos, rope):
    half = rope // 2
    inv_freq = ROPE_THETA ** (-jnp.arange(half, dtype=F32) / half)
    ang = pos.astype(F32)[:, None] * inv_freq[None, :]
    cos, sin = jnp.cos(ang), jnp.sin(ang)
    return {"cos": jnp.concatenate([cos, cos], axis=1), "sin": jnp.concatenate([-sin, sin], axis=1)}


def _block_table(counts, n_rows):
    bm = MOE_ROWS
    padded = (counts + bm - 1) // bm * bm
    ends = jnp.cumsum(padded)
    n_blocks = ends[-1] // bm
    blocks = jnp.arange(n_rows // bm, dtype=I32)
    first_row = jnp.minimum(blocks, n_blocks - 1) * bm
    be = jnp.sum((ends[None, :] <= first_row[:, None]).astype(I32), axis=1)
    n_e = counts.shape[0]
    experts = jnp.arange(n_e, dtype=I32)
    later = (experts[None, :] > experts[:, None]) & (counts[None, :] > 0)
    next_e = jnp.min(jnp.where(later, experts[None, :], n_e), axis=1)
    next_e = jnp.where(next_e < n_e, next_e, -1).astype(I32)
    return ends - padded, jnp.minimum(be, n_e - 1), n_blocks.reshape(1).astype(I32), next_e


def _row_slots(idx, rank, offs):
    experts = jnp.arange(offs.shape[0], dtype=I32)
    return rank + jnp.sum(jnp.where(idx[..., None] == experts, offs, 0), axis=-1)


def kernel(x_prompt, x_sample, cache_kv, cache_kr, state_ssm_re, state_ssm_im, page_table, w_in, q_norm_g, w_uq, kv_norm_g, w_uk, w_uv, ssm_log_dt, ssm_a_re, ssm_a_im, ssm_b_re, ssm_b_im, ssm_c_re, ssm_c_im, ssm_d, ssm_w_glu, ssm_b_glu, gmlp_ln_g, gmlp_ln_b, gmlp_w_s, gmlp_b_s, w_o, ln1_g, ln1_b, router_w, router_bias, exp_w_gate, exp_w_up, exp_w_down, sh_w_gate, sh_w_up, sh_w_down, ln2_g, ln2_b):
    depth = w_in.shape[0]
    batch, seq, d = x_prompt.shape
    dec_b, s_new, _ = x_sample.shape
    n_pages, page = page_table.shape[1], cache_kv.shape[2]
    past = n_pages * page
    tp, ts = batch * seq, dec_b * s_new
    t = tp + ts
    kv_rank, heads, nope = w_uk.shape[1:]
    rope = cache_kr.shape[3]
    n_g, n_p = ssm_a_re.shape[1:]
    n_e = router_w.shape[2]
    assert tp % ROW_TILE == 0 and ts % ROW_TILE == 0 and seq % ROW_TILE == 0 and ROW_TILE % s_new == 0
    assert seq % KV_TILE == 0 and KV_TILE % Q_TILE == 0 and t % MOE_TOKENS == 0 and heads % Q_CHUNK_HEADS == 0
    assert n_pages % PAGE_GROUP == 0 and KEY_CHUNK % page == 0 and (PAGE_GROUP * page) % KEY_CHUNK == 0
    assert dec_b * (n_pages // PAGE_GROUP) >= PAGE_SLOTS
    assert kv_rank + rope <= QK_PAD and s_new <= LANES_V7X
    alpha = (2.0 * depth) ** 0.25
    qscale = (nope + rope) ** -0.5 * math.log2(math.e)
    n_rows = t * TOP_K + n_e * MOE_ROWS
    cache_krt = jnp.swapaxes(cache_kr, 2, 3)

    small = (w_in, q_norm_g, w_uq, kv_norm_g, w_uk, w_uv, ssm_log_dt, ssm_a_re, ssm_a_im, ssm_b_re, ssm_b_im,
             ssm_c_re, ssm_c_im, ssm_d, ssm_w_glu, ssm_b_glu, gmlp_ln_g, gmlp_ln_b, gmlp_w_s, gmlp_b_s, w_o, ln1_g,
             ln1_b, router_w, router_bias, sh_w_gate, sh_w_up, sh_w_down, ln2_g, ln2_b)
    pos = jnp.concatenate([jnp.tile(jnp.arange(seq, dtype=I32), batch),
                           jnp.tile(past + jnp.arange(s_new, dtype=I32), dec_b)])
    tabs = _rope_tables(pos, rope)
    x = jnp.concatenate([x_prompt.reshape(tp, d), x_sample.reshape(ts, d)], axis=0)
    h0_p = jnp.zeros((batch, 1, 2 * n_g * n_p), F32)
    row_sub = d // 2 // LANES_V7X
    xs = jnp.zeros((n_rows * row_sub, LANES_V7X), U32)
    per_layer = []
    for l in range(depth):
        lw = _layer_weights(l, small, s_new)
        cq, c, kr, kcat, u, gm, v = _in_proj(x, lw, tabs, tp // ROW_TILE)
        q_p = _mla_q(cq, lw, tabs, 0, tp, BF16, qscale)
        q_s = _mla_q(cq, lw, tabs, tp, ts, F32, qscale)
        attn_p = _attn_prompt(q_p, kcat, lw["wuv"], batch, seq)
        attn_s = _attn_sample(q_s, c, kr, tp, lw["wuv"], cache_kv, cache_krt, page_table, l)
        h0_s = jnp.concatenate([state_ssm_re[l].reshape(dec_b, 1, -1), state_ssm_im[l].reshape(dec_b, 1, -1)], axis=2)
        ssm_p, hT_p = _s5(u, h0_p, lw, 0, tp, seq)
        ssm_s, hT_s = _s5(u, h0_s, lw, tp, ts, s_new)
        x1, xp = _out_proj(x, attn_p, attn_s, ssm_p, ssm_s, gm, lw, alpha)
        idx, gates, rank, cnt = _router(x1, lw)
        offs, block_e, n_blocks, next_e = _block_table(cnt[:, 0].astype(I32), n_rows)
        dest = _row_slots(idx, rank, offs)
        xs = _dispatch(xp, dest, xs, row_sub)
        yb = _experts(xs, block_e, n_blocks, next_e, exp_w_gate, exp_w_up, exp_w_down, l, row_sub)
        x = _combine(x1, gates, dest, yb, lw, alpha, row_sub)
        per_layer.append((c, kr, hT_p, hT_s, v))

    half = n_g * n_p
    st = lambda f: jnp.stack([f(p) for p in per_layer], axis=0)
    return (
        x[:tp].reshape(batch, seq, d),
        x[tp:].reshape(dec_b, s_new, d),
        st(lambda p: p[0][:tp].reshape(batch, seq, kv_rank)),
        st(lambda p: p[1][:tp].reshape(batch, seq, rope)),
        st(lambda p: p[2][:, 0, :half].reshape(batch, n_g, n_p)),
        st(lambda p: p[2][:, 0, half:].reshape(batch, n_g, n_p)),
        st(lambda p: p[0][tp:].reshape(dec_b, s_new, kv_rank)),
        st(lambda p: p[1][tp:].reshape(dec_b, s_new, rope)),
        st(lambda p: p[3][:, 0, :half].reshape(dec_b, n_g, n_p)),
        st(lambda p: p[3][:, 0, half:].reshape(dec_b, n_g, n_p)),
        st(lambda p: p[4][tp:].reshape(dec_b, s_new, -1)),
    )
```

```python
import functools
import math

import jax
import jax.numpy as jnp
from jax import lax
from jax.experimental import pallas as pl
from jax.experimental.pallas import tpu as pltpu

F32 = jnp.float32
BF16 = jnp.bfloat16
I32 = jnp.int32
U32 = jnp.uint32

LANES_V7X = 128
VMEM_LIMIT_V7X = 56 * 1024 * 1024

ROPE_THETA = 10000.0
EPS = 1e-6
TOP_K = 8
N_EXPERT_GROUPS = 8
TOPK_GROUPS = 4
ROUTE_SCALE = 2.5

ROW_TILE = 256
MOE_ROWS = 256
MOE_TOKENS = 128
Q_TILE = 128
KV_TILE = 256
Q_CHUNK_HEADS = 2
PAGE_GROUP = 16
PAGE_SLOTS = 6
KEY_CHUNK = 1024
QK_PAD = 384


def _params(*sem):
    return pltpu.CompilerParams(dimension_semantics=sem, vmem_limit_bytes=VMEM_LIMIT_V7X)


def _gelu(x):
    c = math.sqrt(2.0 / math.pi)
    return x * (0.5 * (1.0 + jnp.tanh(c * (x + 0.044715 * (x * x * x)))))


def _rms(x, g):
    return x * lax.rsqrt(jnp.mean(x * x, axis=-1, keepdims=True) + EPS) * g


def _layer_norm(x, g, b):
    mu = jnp.mean(x, axis=-1, keepdims=True)
    xc = x - mu
    var = jnp.mean(xc * xc, axis=-1, keepdims=True)
    return xc * lax.rsqrt(var + EPS) * g + b


def _dot(a, b):
    return jnp.dot(a, b, preferred_element_type=F32)


def _dot_nt(a, b):
    return lax.dot_general(a, b, (((1,), (1,)), ((), ())), preferred_element_type=F32)


def _in_proj_kernel(xp_ref, xs_ref, wq_ref, wkv_ref, wkr_ref, wu_ref, wz_ref, gq_ref, gkv_ref, cos_ref, sin_ref,
                    lng_ref, lnb_ref, mix_ref, mixb_ref,
                    cq_ref, c_ref, kr_ref, kcat_ref, u_ref, gm_ref, v_ref,
                    *, kv_rank, rope, g_heads, g_dim, chunk, n_prompt_tiles):
    x_ref = xp_ref
    xb = jnp.where(pl.program_id(0) < n_prompt_tiles, xp_ref[...], xs_ref[...]).astype(BF16)
    cq_ref[...] = _rms(_dot(xb, wq_ref[...]), gq_ref[...]).astype(BF16)
    c = _rms(_dot(xb, wkv_ref[...]), gkv_ref[...])
    c_ref[...] = c
    kr2 = _dot(xb, wkr_ref[...])
    kr = kr2[:, :rope] * cos_ref[...] + kr2[:, rope:] * sin_ref[...]
    kr_ref[...] = kr
    kcat_ref[:, :kv_rank] = c.astype(BF16)
    kcat_ref[:, kv_rank:kv_rank + rope] = kr.astype(BF16)
    kcat_ref[:, kv_rank + rope:] = jnp.zeros((kcat_ref.shape[0], kcat_ref.shape[1] - kv_rank - rope), BF16)
    u_ref[...] = _dot(xb, wu_ref[...])
    z = _gelu(_dot(xb, wz_ref[...]))
    width = g_heads * g_dim
    for h in range(g_heads):
        sl = slice(h * g_dim, (h + 1) * g_dim)
        vh = _layer_norm(z[:, width + h * g_dim: width + (h + 1) * g_dim], lng_ref[:, sl], lnb_ref[:, sl])
        v_ref[:, sl] = vh
        vb = vh.astype(BF16)
        for j in range(x_ref.shape[0] // chunk):
            rows = slice(j * chunk, (j + 1) * chunk)
            sv = _dot(mix_ref[h], vb[rows]) + mixb_ref[:, sl]
            gm_ref[rows, sl] = (z[rows, sl] * sv).astype(BF16)


def _trunk_rows(n_p, n_s):
    return (lambda i: (jnp.minimum(i, n_p - 1), 0)), (lambda i: (jnp.clip(i - n_p, 0, n_s - 1), 0))


def _in_proj(x_p, x_s, lw, tabs):
    d = x_p.shape[1]
    tm = ROW_TILE
    n_prompt_tiles, n_s = x_p.shape[0] // tm, x_s.shape[0] // tm
    t = x_p.shape[0] + x_s.shape[0]
    kv_rank, rope = lw["wkv"].shape[1], lw["wkr"].shape[1] // 2
    gw = lw["lng"].shape[1]
    g_heads, chunk = lw["mix"].shape[1], lw["mix"].shape[2]
    row = lambda i: (i, 0)
    full = lambda i: (0, 0)
    row_p, row_s = _trunk_rows(n_prompt_tiles, n_s)
    trunk = lambda i: (jnp.where(i >= n_prompt_tiles, 1, 0), 0, 0, 0)
    trunk3 = lambda i: (jnp.where(i >= n_prompt_tiles, 1, 0), 0, 0)
    kern = functools.partial(_in_proj_kernel, kv_rank=kv_rank, rope=rope, g_heads=g_heads, g_dim=gw // g_heads,
                             chunk=chunk, n_prompt_tiles=n_prompt_tiles)
    q_rank, ssm_w = lw["wq"].shape[1], lw["wu"].shape[1]
    return pl.pallas_call(
        kern,
        grid=(t // tm,),
        in_specs=[
            pl.BlockSpec((tm, d), row_p), pl.BlockSpec((tm, d), row_s),
            pl.BlockSpec(lw["wq"].shape, full), pl.BlockSpec(lw["wkv"].shape, full),
            pl.BlockSpec(lw["wkr"].shape, full), pl.BlockSpec(lw["wu"].shape, full),
            pl.BlockSpec(lw["wz"].shape, full),
            pl.BlockSpec(lw["gq"].shape, full), pl.BlockSpec(lw["gkv"].shape, full),
            pl.BlockSpec((tm, rope), row), pl.BlockSpec((tm, rope), row),
            pl.BlockSpec(lw["lng"].shape, full), pl.BlockSpec(lw["lnb"].shape, full),
            pl.BlockSpec((None,) + lw["mix"].shape[1:], trunk),
            pl.BlockSpec((None,) + lw["mixb"].shape[1:], trunk3),
        ],
        out_specs=[
            pl.BlockSpec((tm, q_rank), row), pl.BlockSpec((tm, kv_rank), row), pl.BlockSpec((tm, rope), row),
            pl.BlockSpec((tm, QK_PAD), row), pl.BlockSpec((tm, ssm_w), row), pl.BlockSpec((tm, gw), row),
            pl.BlockSpec((tm, gw), row),
        ],
        out_shape=[
            jax.ShapeDtypeStruct((t, q_rank), BF16), jax.ShapeDtypeStruct((t, kv_rank), F32),
            jax.ShapeDtypeStruct((t, rope), F32), jax.ShapeDtypeStruct((t, QK_PAD), BF16),
            jax.ShapeDtypeStruct((t, ssm_w), F32), jax.ShapeDtypeStruct((t, gw), BF16),
            jax.ShapeDtypeStruct((t, gw), F32),
        ],
        compiler_params=_params("parallel"),
        name="in_proj",
    )(x_p, x_s, lw["wq"], lw["wkv"], lw["wkr"], lw["wu"], lw["wz"], lw["gq"], lw["gkv"], tabs["cos"], tabs["sin"],
      lw["lng"], lw["lnb"], lw["mix"], lw["mixb"])


def _mla_q_kernel(cq_ref, wqn_ref, wqr_ref, wuk_ref, cos_ref, sin_ref, q_ref, *, kv_rank, rope, qscale):
    cq = cq_ref[...]
    for h in range(q_ref.shape[0]):
        qn = _dot(cq, wqn_ref[h]).astype(BF16)
        q_ref[h, :, :kv_rank] = (_dot(qn, wuk_ref[h]) * qscale).astype(q_ref.dtype)
        qr2 = _dot(cq, wqr_ref[h])
        qr = qr2[:, :rope] * cos_ref[...] + qr2[:, rope:] * sin_ref[...]
        q_ref[h, :, kv_rank:kv_rank + rope] = (qr * qscale).astype(q_ref.dtype)
        q_ref[h, :, kv_rank + rope:] = jnp.zeros((q_ref.shape[1], q_ref.shape[2] - kv_rank - rope), q_ref.dtype)


def _mla_q(cq, lw, tabs, row0, n_rows, out_dtype, qscale):
    tm = ROW_TILE
    heads, q_rank, _ = lw["wqn"].shape
    kv_rank, rope = lw["wuk"].shape[2], lw["wqr"].shape[2] // 2
    off = row0 // tm
    row = lambda i: (i + off, 0)
    full = lambda i: (0, 0, 0)
    return pl.pallas_call(
        functools.partial(_mla_q_kernel, kv_rank=kv_rank, rope=rope, qscale=qscale),
        grid=(n_rows // tm,),
        in_specs=[
            pl.BlockSpec((tm, q_rank), row),
            pl.BlockSpec(lw["wqn"].shape, full), pl.BlockSpec(lw["wqr"].shape, full),
            pl.BlockSpec(lw["wuk"].shape, full),
            pl.BlockSpec((tm, rope), row), pl.BlockSpec((tm, rope), row),
        ],
        out_specs=pl.BlockSpec((heads, tm, QK_PAD), lambda i: (0, i, 0)),
        out_shape=jax.ShapeDtypeStruct((heads, n_rows, QK_PAD), out_dtype),
        compiler_params=_params("parallel"),
        name="mla_q",
    )(cq, lw["wqn"], lw["wqr"], lw["wuk"], tabs["cos"], tabs["sin"])


def _lane_tiles(x, op):
    tiles = [x[:, i * LANES_V7X:(i + 1) * LANES_V7X] for i in range(x.shape[1] // LANES_V7X)]
    return functools.reduce(op, tiles)


def _widen(x, width):
    return jnp.tile(x, (1, width // LANES_V7X))


def _attn_prompt_kernel(q_ref, k_ref, wuv_ref, o_ref, s_scr, m_scr, l_scr, acc_scr, *, tq, tk, kv_rank, hpc):
    qi = pl.program_id(1)
    heads = q_ref.shape[0]
    cr = hpc * tq
    q = q_ref[...].reshape(heads * tq, q_ref.shape[2])
    m_scr[...] = jnp.full(m_scr.shape, -jnp.inf, F32)
    l_scr[...] = jnp.zeros(l_scr.shape, F32)
    acc_scr[...] = jnp.zeros(acc_scr.shape, F32)
    n_full = (qi * tq) // tk

    def keys(j):
        return k_ref[pl.ds(pl.multiple_of(j * tk, tk), tk), :]

    def scores(j, masked):
        s = _dot_nt(q, keys(j))
        if masked:
            qpos = qi * tq + lax.broadcasted_iota(I32, s.shape, 0) % tq
            kpos = j * tk + lax.broadcasted_iota(I32, s.shape, 1)
            s = jnp.where(kpos <= qpos, s, -jnp.inf)
        s_scr[j % 2] = s

    def weigh(j):
        v = keys(j)[:, :kv_rank]
        for c in range(heads // hpc):
            rows = slice(c * cr, (c + 1) * cr)
            s = s_scr[j % 2, rows]
            m_prev = m_scr[rows]
            m_new = jnp.maximum(m_prev, jnp.max(s, axis=-1, keepdims=True))
            alpha = jnp.exp2(m_prev - m_new)
            p = jnp.exp2(s - _widen(m_new, tk))
            l_scr[rows] = alpha * l_scr[rows] + _lane_tiles(p, jnp.add)
            acc_scr[rows] = _widen(alpha, kv_rank) * acc_scr[rows] + _dot(p.astype(BF16), v)
            m_scr[rows] = m_new

    @pl.when(n_full == 0)
    def _():
        scores(0, True)

    @pl.when(n_full > 0)
    def _():
        scores(0, False)

        def step(j, carry):
            weigh(j)
            scores(j + 1, False)
            return carry

        lax.fori_loop(0, n_full - 1, step, 0)
        weigh(n_full - 1)
        scores(n_full, True)

    weigh(n_full)
    o = (acc_scr[...] / jnp.sum(l_scr[...], axis=-1, keepdims=True)).astype(BF16)
    v_dim = wuv_ref.shape[2]
    for h in range(heads):
        o_ref[:, h * v_dim:(h + 1) * v_dim] = _dot(o[h * tq:(h + 1) * tq], wuv_ref[h])


def _attn_prompt(q, kcat, wuv, batch, seq):
    heads, tp, _ = q.shape
    kv_rank, v_dim = wuv.shape[1], wuv.shape[2]
    tq, tk = Q_TILE, KV_TILE
    nq = seq // tq
    rows = heads * tq
    return pl.pallas_call(
        functools.partial(_attn_prompt_kernel, tq=tq, tk=tk, kv_rank=kv_rank, hpc=Q_CHUNK_HEADS),
        grid=(batch, nq),
        in_specs=[
            pl.BlockSpec((heads, tq, QK_PAD), lambda b, i: (0, b * nq + i, 0)),
            pl.BlockSpec((seq, QK_PAD), lambda b, i: (b, 0)),
            pl.BlockSpec(wuv.shape, lambda b, i: (0, 0, 0)),
        ],
        out_specs=pl.BlockSpec((tq, heads * v_dim), lambda b, i: (b * nq + i, 0)),
        out_shape=jax.ShapeDtypeStruct((tp, heads * v_dim), F32),
        scratch_shapes=[pltpu.VMEM((2, rows, tk), F32), pltpu.VMEM((rows, LANES_V7X), F32),
                        pltpu.VMEM((rows, LANES_V7X), F32), pltpu.VMEM((rows, kv_rank), F32)],
        compiler_params=_params("parallel", "arbitrary"),
        name="attn_prompt",
    )(q, kcat, wuv)


def _attn_sample_kernel(pt_ref, q_ref, cnew_ref, krnew_ref, wuv_ref, ckv_hbm, ckrt_hbm, o_ref,
                        kvbuf, krbuf, kb_all, krt_scr, knew_scr, s_scr, sems,
                        *, layer, n_pages, group, page, kv_rank, rope, chunk):
    b = pl.program_id(0)
    heads, s_new, _ = q_ref.shape
    rows = heads * s_new
    n_groups = n_pages // group
    ppc = chunk // page
    cpg = group // ppc
    q = q_ref[...].reshape(rows, q_ref.shape[2]).astype(BF16)
    q_lat, q_rope = q[:, :kv_rank], q[:, kv_rank:]

    n_slots = kvbuf.shape[0]
    ahead = n_slots - 1
    total = pl.num_programs(0) * n_groups

    def copies(gg, slot, p):
        pg = pt_ref[gg * group + p]
        return (pltpu.make_async_copy(ckv_hbm.at[layer, pg], kvbuf.at[slot, p], sems.at[0, slot]),
                pltpu.make_async_copy(ckrt_hbm.at[layer, pg], krbuf.at[slot, p], sems.at[1, slot]))

    def start(gg):
        for p in range(group):
            for cp in copies(gg, gg % n_slots, p):
                cp.start()

    def wait(slot):
        for p in range(group):
            for cp in copies(0, slot, p):
                cp.wait()

    @pl.when(b == 0)
    def _():
        for gg in range(ahead):
            start(gg)

    krt_scr[:, rope:, :] = jnp.zeros((cpg, krt_scr.shape[1] - rope, chunk), BF16)

    def stream(g, m_part):
        gg = b * n_groups + g
        slot = gg % n_slots

        @pl.when(gg + ahead < total)
        def _():
            start(gg + ahead)

        wait(slot)
        for c in range(cpg):
            kvb = kvbuf[slot, c * ppc:(c + 1) * ppc].reshape(chunk, kv_rank).astype(BF16)
            kb_all[g * cpg + c] = kvb
            for p in range(ppc):
                krt_scr[c, :rope, p * page:(p + 1) * page] = krbuf[slot, c * ppc + p].astype(BF16)
            s = _dot_nt(q_lat, kvb) + _dot(q_rope, krt_scr[c])
            s_scr[g * cpg + c] = s
            m_part = jnp.maximum(m_part, _lane_tiles(s, jnp.maximum))
        return m_part

    m_part = lax.fori_loop(0, n_groups, stream, jnp.full((rows, LANES_V7X), -jnp.inf, F32))

    knew_scr[...] = jnp.zeros(knew_scr.shape, BF16)
    knew_scr[:s_new, :kv_rank] = cnew_ref[...].astype(BF16)
    knew_scr[:s_new, kv_rank:kv_rank + rope] = krnew_ref[...].astype(BF16)
    kn = knew_scr[...]
    sn = _dot_nt(q, kn)
    qpos = lax.broadcasted_iota(I32, sn.shape, 0) % s_new
    kpos = lax.broadcasted_iota(I32, sn.shape, 1)
    sn = jnp.where(kpos <= qpos, sn, -jnp.inf)
    m = jnp.max(jnp.maximum(m_part, sn), axis=-1, keepdims=True)
    m_wide = jnp.broadcast_to(m, (rows, chunk))

    def weigh(i, carry):
        l_part, acc = carry
        p = jnp.exp2(s_scr[i] - m_wide)
        return l_part + _lane_tiles(p, jnp.add), acc + _dot(p.astype(BF16), kb_all[i])

    pn = jnp.exp2(sn - m_wide[:, :LANES_V7X])
    l_part, acc = lax.fori_loop(0, n_groups * cpg, weigh, (pn, _dot(pn.astype(BF16), kn[:, :kv_rank])), unroll=2)
    o = (acc / jnp.sum(l_part, axis=-1, keepdims=True)).astype(BF16)
    v_dim = wuv_ref.shape[2]
    for h in range(heads):
        o_ref[:, h * v_dim:(h + 1) * v_dim] = _dot(o, wuv_ref[h])[h * s_new:(h + 1) * s_new]


def _attn_sample(q, c, kr, row0, wuv, cache_kv, cache_krt, page_table, layer):
    heads, ts, _ = q.shape
    dec_b, n_pages = page_table.shape
    s_new = ts // dec_b
    page, kv_rank = cache_kv.shape[2], cache_kv.shape[3]
    rope = cache_krt.shape[2]
    v_dim = wuv.shape[2]
    group, chunk = PAGE_GROUP, KEY_CHUNK
    n_chunks = n_pages * page // chunk
    rows = heads * s_new
    off = row0 // s_new
    grid_spec = pltpu.PrefetchScalarGridSpec(
        num_scalar_prefetch=1,
        grid=(dec_b,),
        in_specs=[
            pl.BlockSpec((heads, s_new, QK_PAD), lambda b, pt: (0, b, 0)),
            pl.BlockSpec((s_new, kv_rank), lambda b, pt: (b + off, 0)),
            pl.BlockSpec((s_new, rope), lambda b, pt: (b + off, 0)),
            pl.BlockSpec(wuv.shape, lambda b, pt: (0, 0, 0)),
            pl.BlockSpec(memory_space=pl.ANY),
            pl.BlockSpec(memory_space=pl.ANY),
        ],
        out_specs=pl.BlockSpec((s_new, heads * v_dim), lambda b, pt: (b, 0)),
        scratch_shapes=[
            pltpu.VMEM((PAGE_SLOTS, group, page, kv_rank), F32),
            pltpu.VMEM((PAGE_SLOTS, group, rope, page), F32),
            pltpu.VMEM((n_chunks, chunk, kv_rank), BF16),
            pltpu.VMEM((group * page // chunk, QK_PAD - kv_rank, chunk), BF16),
            pltpu.VMEM((LANES_V7X, QK_PAD), BF16),
            pltpu.VMEM((n_chunks, rows, chunk), F32),
            pltpu.SemaphoreType.DMA((2, PAGE_SLOTS)),
        ],
    )
    return pl.pallas_call(
        functools.partial(_attn_sample_kernel, layer=layer, n_pages=n_pages, group=group, page=page,
                          kv_rank=kv_rank, rope=rope, chunk=chunk),
        grid_spec=grid_spec,
        out_shape=jax.ShapeDtypeStruct((ts, heads * v_dim), F32),
        compiler_params=_params("arbitrary"),
        name="attn_sample",
    )(page_table.reshape(-1), q, c, kr, wuv, cache_kv, cache_krt)


def _s5_kernel(u_ref, h0_ref, wbu_ref, a_ref, wc_ref, d_ref, wglu_ref, bglu_ref, y_ref, ht_ref, hs_scr, carry_scr,
               *, seq_rows, seqs_per_tile, tiles_per_seq):
    i = pl.program_id(0)
    u = u_ref[...]
    hs_scr[...] = _dot(u.astype(BF16), wbu_ref[...])
    n = a_ref.shape[1]
    a_re, a_im = a_ref[0:1, :], a_ref[1:2, :]

    if tiles_per_seq > 1:
        @pl.when(i % tiles_per_seq == 0)
        def _():
            carry_scr[...] = h0_ref[0]

    def one_sequence(s, carry):
        h = carry_scr[...] if tiles_per_seq > 1 else h0_ref[s]

        def row(t, hc):
            hr, hi = hc
            r = s * seq_rows + t
            nr = a_re * hr - a_im * hi + hs_scr[pl.ds(r, 1), :n]
            ni = a_re * hi + a_im * hr + hs_scr[pl.ds(r, 1), n:]
            hs_scr[pl.ds(r, 1), :n] = nr
            hs_scr[pl.ds(r, 1), n:] = ni
            return nr, ni

        hr, hi = lax.fori_loop(0, seq_rows, row, (h[:, :n], h[:, n:]), unroll=8)
        last = jnp.concatenate([hr, hi], axis=1)
        ht_ref[s] = last
        if tiles_per_seq > 1:
            carry_scr[...] = last
        return carry

    lax.fori_loop(0, seqs_per_tile, one_sequence, 0)
    y = _dot(hs_scr[...].astype(BF16), wc_ref[...]) + d_ref[...] * u
    g = _gelu(y)
    y_ref[...] = g * jax.nn.sigmoid(_dot(g.astype(BF16), wglu_ref[...]) + bglu_ref[...])


def _s5(u, h0, lw, row0, n_rows, seq_len):
    tc = ROW_TILE
    width = u.shape[1]
    n2 = lw["wbu"].shape[1]
    seq_rows = min(seq_len, tc)
    seqs_per_tile = tc // seq_rows
    tiles_per_seq = max(1, seq_len // tc)
    n_seq = n_rows // seq_len
    off = row0 // tc
    full = lambda i: (0, 0)
    seq_idx = (lambda i: (i // tiles_per_seq, 0, 0)) if tiles_per_seq > 1 else (lambda i: (i, 0, 0))
    return pl.pallas_call(
        functools.partial(_s5_kernel, seq_rows=seq_rows, seqs_per_tile=seqs_per_tile, tiles_per_seq=tiles_per_seq),
        grid=(n_rows // tc,),
        in_specs=[
            pl.BlockSpec((tc, width), lambda i: (i + off, 0)),
            pl.BlockSpec((seqs_per_tile, 1, n2), seq_idx),
            pl.BlockSpec(lw["wbu"].shape, full), pl.BlockSpec(lw["a"].shape, full),
            pl.BlockSpec(lw["wc"].shape, full), pl.BlockSpec(lw["d"].shape, full),
            pl.BlockSpec(lw["wglu"].shape, full), pl.BlockSpec(lw["bglu"].shape, full),
        ],
        out_specs=[pl.BlockSpec((tc, width), lambda i: (i, 0)), pl.BlockSpec((seqs_per_tile, 1, n2), seq_idx)],
        out_shape=[jax.ShapeDtypeStruct((n_rows, width), F32), jax.ShapeDtypeStruct((n_seq, 1, n2), F32)],
        scratch_shapes=[pltpu.VMEM((tc, n2), F32), pltpu.VMEM((1, n2), F32)],
        compiler_params=_params("arbitrary"),
        name="s5",
    )(u, h0, lw["wbu"], lw["a"], lw["wc"], lw["d"], lw["wglu"], lw["bglu"])


def _pack_bf16_pairs(x):
    half = x.shape[1] // 2
    hi = pltpu.bitcast(x[:, :half].astype(BF16).astype(F32), U32)
    lo = pltpu.bitcast(x[:, half:].astype(BF16).astype(F32), U32)
    return hi | (lo >> 16)


def _unpack_bf16_pairs(p):
    hi = pltpu.bitcast(p & jnp.uint32(0xFFFF0000), F32)
    lo = pltpu.bitcast(p << 16, F32)
    return hi, lo


def _store_row_tiles(ref, x):
    rows, n = x.shape[0], x.shape[1] // LANES_V7X
    for j in range(n):
        ref[pl.ds(j, rows, stride=n), :] = x[:, j * LANES_V7X:(j + 1) * LANES_V7X]


def _load_row_tiles(ref, rows):
    n = ref.shape[0] // rows
    return jnp.concatenate([ref[pl.ds(j, rows, stride=n), :] for j in range(n)], axis=1)


def _out_proj_kernel(xa_ref, xb_ref, ap_ref, as_ref, sp_ref, ss_ref, g_ref, woa_ref, wos_ref, wog_ref, lg_ref, lb_ref,
                     x1_ref, xp_ref, *, alpha, n_prompt_tiles):
    prompt = pl.program_id(0) < n_prompt_tiles
    x = jnp.where(prompt, xa_ref[...], xb_ref[...])
    a = jnp.where(prompt, ap_ref[...], as_ref[...])
    s = jnp.where(prompt, sp_ref[...], ss_ref[...])
    mix = _dot(a.astype(BF16), woa_ref[...]) + _dot(s.astype(BF16), wos_ref[...]) + _dot(g_ref[...], wog_ref[...])
    x1 = _layer_norm(alpha * x + mix, lg_ref[...], lb_ref[...])
    x1_ref[...] = x1
    _store_row_tiles(xp_ref, _pack_bf16_pairs(x1))


def _out_proj(x_p, x_s, attn_p, attn_s, ssm_p, ssm_s, gm, lw, alpha):
    d = x_p.shape[1]
    t = x_p.shape[0] + x_s.shape[0]
    tm = ROW_TILE
    n_p, n_s = attn_p.shape[0] // tm, attn_s.shape[0] // tm
    row_sub = d // 2 // LANES_V7X
    row = lambda i: (i, 0)
    full = lambda i: (0, 0)
    row_p, row_s = _trunk_rows(n_p, n_s)
    return pl.pallas_call(
        functools.partial(_out_proj_kernel, alpha=alpha, n_prompt_tiles=n_p),
        grid=(t // tm,),
        in_specs=[
            pl.BlockSpec((tm, d), row_p), pl.BlockSpec((tm, d), row_s),
            pl.BlockSpec((tm, attn_p.shape[1]), row_p), pl.BlockSpec((tm, attn_s.shape[1]), row_s),
            pl.BlockSpec((tm, ssm_p.shape[1]), row_p), pl.BlockSpec((tm, ssm_s.shape[1]), row_s),
            pl.BlockSpec((tm, gm.shape[1]), row),
            pl.BlockSpec(lw["woa"].shape, full), pl.BlockSpec(lw["wos"].shape, full), pl.BlockSpec(lw["wog"].shape, full),
            pl.BlockSpec(lw["ln1g"].shape, full), pl.BlockSpec(lw["ln1b"].shape, full),
        ],
        out_specs=[pl.BlockSpec((tm, d), row), pl.BlockSpec((tm * row_sub, LANES_V7X), row)],
        out_shape=[jax.ShapeDtypeStruct((t, d), F32), jax.ShapeDtypeStruct((t * row_sub, LANES_V7X), U32)],
        compiler_params=_params("parallel"),
        name="out_proj",
    )(x_p, x_s, attn_p, attn_s, ssm_p, ssm_s, gm, lw["woa"], lw["wos"], lw["wog"], lw["ln1g"], lw["ln1b"])


def _first_index_of_max(x, iota, n):
    m = jnp.max(x, axis=0, keepdims=True)
    return m, jnp.min(jnp.where(x == m, iota, n), axis=0, keepdims=True)


def _router_kernel(x_ref, rwt_ref, bias_ref, idx_ref, gate_ref, rank_ref, cnt_ref, carry_scr):
    i = pl.program_id(0)

    @pl.when(i == 0)
    def _():
        carry_scr[...] = jnp.zeros(carry_scr.shape, F32)

    n_e = rwt_ref.shape[0]
    tm = x_ref.shape[0]
    per_group = n_e // N_EXPERT_GROUPS
    logits = lax.dot_general(rwt_ref[...], x_ref[...], (((1,), (1,)), ((), ())),
                             precision=lax.Precision.HIGHEST, preferred_element_type=F32)
    scores = jax.nn.sigmoid(logits)
    sel = scores + bias_ref[...]
    neg = -jnp.inf
    sub = lax.broadcasted_iota(I32, (per_group, tm), 0)
    group_scores = []
    for g in range(N_EXPERT_GROUPS):
        blk = sel[g * per_group:(g + 1) * per_group, :]
        m1, i1 = _first_index_of_max(blk, sub, per_group)
        m2 = jnp.max(jnp.where(sub == i1, neg, blk), axis=0, keepdims=True)
        group_scores.append(m1 + m2)
    work = jnp.concatenate(group_scores, axis=0)
    g_iota = lax.broadcasted_iota(I32, work.shape, 0)
    keep = jnp.zeros(work.shape, F32)
    for _ in range(TOPK_GROUPS):
        _, gi = _first_index_of_max(work, g_iota, N_EXPERT_GROUPS)
        hit = g_iota == gi
        keep = jnp.where(hit, 1.0, keep)
        work = jnp.where(hit, neg, work)
    keep_e = jnp.concatenate([jnp.broadcast_to(keep[g:g + 1, :], (per_group, tm)) for g in range(N_EXPERT_GROUPS)], axis=0)
    cand = jnp.where(keep_e > 0.0, sel, neg)
    e_iota = lax.broadcasted_iota(I32, (n_e, tm), 0)
    onehot = jnp.zeros((n_e, tm), F32)
    idx_rows, gate_rows = [], []
    for _ in range(TOP_K):
        _, ik = _first_index_of_max(cand, e_iota, n_e)
        hit = e_iota == ik
        gate_rows.append(jnp.sum(jnp.where(hit, scores, 0.0), axis=0, keepdims=True))
        onehot = jnp.where(hit, 1.0, onehot)
        cand = jnp.where(hit, neg, cand)
        idx_rows.append(ik)
    gates = jnp.concatenate(gate_rows, axis=0)
    gate_ref[...] = gates / jnp.sum(gates, axis=0, keepdims=True) * ROUTE_SCALE
    idx_ref[...] = jnp.concatenate(idx_rows, axis=0)
    before = (lax.broadcasted_iota(I32, (tm, tm), 0) < lax.broadcasted_iota(I32, (tm, tm), 1)).astype(BF16)
    prefix = _dot(onehot.astype(BF16), before) + carry_scr[:, :1]
    rank_rows = [jnp.sum(jnp.where(e_iota == ik, prefix, 0.0), axis=0, keepdims=True) for ik in idx_rows]
    rank_ref[...] = jnp.concatenate(rank_rows, axis=0).astype(I32)
    carry_scr[...] = carry_scr[...] + jnp.sum(onehot, axis=1, keepdims=True)
    cnt_ref[...] = carry_scr[...]


def _router(x1, lw):
    t, d = x1.shape
    tm = ROW_TILE
    n_e = lw["rwt"].shape[0]
    col = lambda i: (0, i)
    full = lambda i: (0, 0)
    return pl.pallas_call(
        _router_kernel,
        grid=(t // tm,),
        in_specs=[pl.BlockSpec((tm, d), lambda i: (i, 0)), pl.BlockSpec(lw["rwt"].shape, full),
                  pl.BlockSpec(lw["rbias"].shape, full)],
        out_specs=[pl.BlockSpec((TOP_K, tm), col), pl.BlockSpec((TOP_K, tm), col), pl.BlockSpec((TOP_K, tm), col),
                   pl.BlockSpec((n_e, LANES_V7X), full)],
        out_shape=[jax.ShapeDtypeStruct((TOP_K, t), I32), jax.ShapeDtypeStruct((TOP_K, t), F32),
                   jax.ShapeDtypeStruct((TOP_K, t), I32), jax.ShapeDtypeStruct((n_e, LANES_V7X), F32)],
        scratch_shapes=[pltpu.VMEM((n_e, LANES_V7X), F32)],
        compiler_params=_params("arbitrary"),
        name="router",
    )(x1, lw["rwt"], lw["rbias"])


def _dispatch_kernel(dest_ref, xp_ref, xs_in, xs_hbm, sem, *, row_sub):
    del xs_in
    tokens = xp_ref.shape[0] // row_sub

    def copy(tok, k):
        src = pl.ds(pl.multiple_of(tok * row_sub, row_sub), row_sub)
        dst = pl.ds(pl.multiple_of(dest_ref[k, tok] * row_sub, row_sub), row_sub)
        return pltpu.make_async_copy(xp_ref.at[src, :], xs_hbm.at[dst, :], sem)

    def start(tok, carry):
        for k in range(TOP_K):
            copy(tok, k).start()
        return carry

    def wait(tok, carry):
        for k in range(TOP_K):
            copy(tok, k).wait()
        return carry

    lax.fori_loop(0, tokens, start, 0)
    lax.fori_loop(0, tokens, wait, 0)


def _dispatch(xp, dest, xs_buf, row_sub):
    tm = MOE_TOKENS
    t = xp.shape[0] // row_sub
    return pl.pallas_call(
        functools.partial(_dispatch_kernel, row_sub=row_sub),
        grid=(t // tm,),
        in_specs=[pl.BlockSpec((TOP_K, tm), lambda i: (0, i), memory_space=pltpu.SMEM),
                  pl.BlockSpec((tm * row_sub, LANES_V7X), lambda i: (i, 0)),
                  pl.BlockSpec(memory_space=pl.ANY)],
        out_specs=pl.BlockSpec(memory_space=pl.ANY),
        out_shape=jax.ShapeDtypeStruct(xs_buf.shape, U32),
        scratch_shapes=[pltpu.SemaphoreType.DMA(())],
        input_output_aliases={2: 0},
        compiler_params=_params("arbitrary"),
        name="moe_dispatch",
    )(dest, xp, xs_buf)


def _experts_kernel(be_ref, nb_ref, nxt_ref, xs_ref, wg_hbm, wu_hbm, wd_hbm, y_ref, wg_f, wu_f, wd_f, wg_b, wu_b, wd_b,
                    sems, *, layer, rows):
    i = pl.program_id(0)
    e = be_ref[i]
    fresh = jnp.logical_or(i == 0, e != be_ref[jnp.maximum(i - 1, 0)])

    def copies(ex):
        return (pltpu.make_async_copy(wg_hbm.at[layer, ex], wg_f, sems.at[0]),
                pltpu.make_async_copy(wu_hbm.at[layer, ex], wu_f, sems.at[1]),
                pltpu.make_async_copy(wd_hbm.at[layer, ex], wd_f, sems.at[2]))

    @pl.when(i == 0)
    def _():
        for cp in copies(e):
            cp.start()

    @pl.when(fresh)
    def _():
        for cp in copies(e):
            cp.wait()
        wg_b[...] = wg_f[...].astype(BF16)
        wu_b[...] = wu_f[...].astype(BF16)
        wd_b[...] = wd_f[...].astype(BF16)

        @pl.when(nxt_ref[e] >= 0)
        def _():
            for cp in copies(nxt_ref[e]):
                cp.start()

    @pl.when(i < nb_ref[0])
    def _():
        hi, lo = _unpack_bf16_pairs(_load_row_tiles(xs_ref, rows))
        hi, lo = hi.astype(BF16), lo.astype(BF16)
        half = hi.shape[1]
        g = _dot(hi, wg_b[:half, :]) + _dot(lo, wg_b[half:, :])
        u = _dot(hi, wu_b[:half, :]) + _dot(lo, wu_b[half:, :])
        y = _dot((jax.nn.silu(g) * u).astype(BF16), wd_b[...])
        _store_row_tiles(y_ref, _pack_bf16_pairs(y))

    @pl.when(i >= nb_ref[0])
    def _():
        y_ref[...] = jnp.zeros(y_ref.shape, U32)


def _experts(xs, block_e, n_blocks, next_e, wg, wu, wd, layer, row_sub):
    bm = MOE_ROWS
    n_rows = xs.shape[0] // row_sub
    d, ff = wg.shape[2], wg.shape[3]
    blk = (bm * row_sub, LANES_V7X)
    grid_spec = pltpu.PrefetchScalarGridSpec(
        num_scalar_prefetch=3,
        grid=(n_rows // bm,),
        in_specs=[
            pl.BlockSpec(blk, lambda i, be, nb, nx: (jnp.minimum(i, nb[0] - 1), 0)),
            pl.BlockSpec(memory_space=pl.ANY), pl.BlockSpec(memory_space=pl.ANY), pl.BlockSpec(memory_space=pl.ANY),
        ],
        out_specs=pl.BlockSpec(blk, lambda i, be, nb, nx: (i, 0)),
        scratch_shapes=[pltpu.VMEM((d, ff), F32), pltpu.VMEM((d, ff), F32), pltpu.VMEM((ff, d), F32),
                        pltpu.VMEM((d, ff), BF16), pltpu.VMEM((d, ff), BF16), pltpu.VMEM((ff, d), BF16),
                        pltpu.SemaphoreType.DMA((3,))],
    )
    return pl.pallas_call(
        functools.partial(_experts_kernel, layer=layer, rows=bm),
        grid_spec=grid_spec,
        out_shape=jax.ShapeDtypeStruct(xs.shape, U32),
        compiler_params=_params("arbitrary"),
        name="moe_experts",
    )(block_e, n_blocks, next_e, xs, wg, wu, wd)


def _combine_kernel(dest_ref, dnext_ref, x1_ref, gate_ref, sg_ref, su_ref, sd_ref, lg_ref, lb_ref, yb_hbm,
                    x2p_ref, x2s_ref, buf, sems, *, alpha, row_sub, n_prompt_tiles):
    i = pl.program_id(0)
    tokens = x1_ref.shape[0]
    slot = i % 2

    def copy(dref, s, tok, k):
        src = pl.ds(pl.multiple_of(dref[k, tok] * row_sub, row_sub), row_sub)
        dst = pl.ds(pl.multiple_of(tok * row_sub, row_sub), row_sub)
        return pltpu.make_async_copy(yb_hbm.at[src, :], buf.at[s, k, dst, :], sems.at[s])

    def gather(dref, s):
        def start(tok, carry):
            for k in range(TOP_K):
                copy(dref, s, tok, k).start()
            return carry

        lax.fori_loop(0, tokens, start, 0)

    def wait(tok, carry):
        for k in range(TOP_K):
            copy(dest_ref, slot, tok, k).wait()
        return carry

    @pl.when(i == 0)
    def _():
        gather(dest_ref, 0)

    @pl.when(i + 1 < pl.num_programs(0))
    def _():
        gather(dnext_ref, 1 - slot)

    x1 = x1_ref[...]
    xb = x1.astype(BF16)
    h = (jax.nn.silu(_dot(xb, sg_ref[...])) * _dot(xb, su_ref[...])).astype(BF16)
    acc = alpha * x1 + _dot(h, sd_ref[...])
    gates = jnp.concatenate([gate_ref[...], jnp.zeros((tokens - TOP_K, tokens), F32)], axis=0).T
    lax.fori_loop(0, tokens, wait, 0)
    half = x1.shape[1] // 2
    routed_hi = jnp.zeros((tokens, half), F32)
    routed_lo = jnp.zeros((tokens, half), F32)
    for k in range(TOP_K):
        hi, lo = _unpack_bf16_pairs(_load_row_tiles(buf.at[slot, k], tokens))
        routed_hi = routed_hi + gates[:, k:k + 1] * hi
        routed_lo = routed_lo + gates[:, k:k + 1] * lo
    acc = acc + jnp.concatenate([routed_hi, routed_lo], axis=1)
    x2 = _layer_norm(acc, lg_ref[...], lb_ref[...])

    @pl.when(i < n_prompt_tiles)
    def _():
        x2p_ref[...] = x2

    @pl.when(i >= n_prompt_tiles)
    def _():
        x2s_ref[...] = x2


def _combine(x1, gates, dest, yb, lw, alpha, row_sub, tp):
    t, d = x1.shape
    tm = MOE_TOKENS
    n = t // tm
    n_p = tp // tm
    full = lambda i: (0, 0)
    row_p, row_s = _trunk_rows(n_p, n - n_p)
    return pl.pallas_call(
        functools.partial(_combine_kernel, alpha=alpha, row_sub=row_sub, n_prompt_tiles=n_p),
        grid=(n,),
        in_specs=[
            pl.BlockSpec((TOP_K, tm), lambda i: (0, i), memory_space=pltpu.SMEM),
            pl.BlockSpec((TOP_K, tm), lambda i: (0, jnp.minimum(i + 1, n - 1)), memory_space=pltpu.SMEM),
            pl.BlockSpec((tm, d), lambda i: (i, 0)),
            pl.BlockSpec((TOP_K, tm), lambda i: (0, i)),
            pl.BlockSpec(lw["sg"].shape, full), pl.BlockSpec(lw["su"].shape, full), pl.BlockSpec(lw["sd"].shape, full),
            pl.BlockSpec(lw["ln2g"].shape, full), pl.BlockSpec(lw["ln2b"].shape, full),
            pl.BlockSpec(memory_space=pl.ANY),
        ],
        out_specs=[pl.BlockSpec((tm, d), row_p), pl.BlockSpec((tm, d), row_s)],
        out_shape=[jax.ShapeDtypeStruct((tp, d), F32), jax.ShapeDtypeStruct((t - tp, d), F32)],
        scratch_shapes=[pltpu.VMEM((2, TOP_K, tm * row_sub, LANES_V7X), U32), pltpu.SemaphoreType.DMA((2,))],
        compiler_params=_params("arbitrary"),
        name="moe_combine",
    )(dest, dest, x1, gates, lw["sg"], lw["su"], lw["sd"], lw["ln2g"], lw["ln2b"], yb)


def _swap_halves(w):
    half = w.shape[-1] // 2
    return jnp.concatenate([w[..., half:], w[..., :half]], axis=-1)


def _s5_discretise(log_dt, a_re, a_im, b_re, b_im, c_re, c_im):
    n_g, n_p = a_re.shape
    n_h = b_re.shape[2]
    dt = jnp.exp(log_dt)[:, None]
    mag = jnp.exp(a_re * dt)
    ab_re, ab_im = mag * jnp.cos(a_im * dt), mag * jnp.sin(a_im * dt)
    den = jnp.square(a_re) + jnp.square(a_im)
    f_re = ((ab_re - 1.0) * a_re + ab_im * a_im) / den
    f_im = (ab_im * a_re - (ab_re - 1.0) * a_im) / den
    bb_re = f_re[..., None] * b_re - f_im[..., None] * b_im
    bb_im = f_re[..., None] * b_im + f_im[..., None] * b_re
    eye = jnp.eye(n_g, dtype=F32)
    blk = lambda m: jnp.einsum("gph,gk->ghkp", m, eye).reshape(n_g * n_h, n_g * n_p)
    wbu = jnp.concatenate([blk(bb_re), blk(bb_im)], axis=1)
    blk_c = lambda m: jnp.einsum("ghp,gk->kpgh", m, eye).reshape(n_g * n_p, n_g * n_h)
    wc = jnp.concatenate([blk_c(c_re), blk_c(-c_im)], axis=0)
    a = jnp.stack([ab_re.reshape(-1), ab_im.reshape(-1)], axis=0)
    return a, wbu.astype(BF16), wc.astype(BF16)


def _layer_weights(l, w, s_new):
    (w_in, q_norm_g, w_uq, kv_norm_g, w_uk, w_uv, ssm_log_dt, ssm_a_re, ssm_a_im, ssm_b_re, ssm_b_im, ssm_c_re,
     ssm_c_im, ssm_d, ssm_w_glu, ssm_b_glu, gmlp_ln_g, gmlp_ln_b, gmlp_w_s, gmlp_b_s, w_o, ln1_g, ln1_b, router_w,
     router_bias, sh_w_gate, sh_w_up, sh_w_down, ln2_g, ln2_b) = [a[l] for a in w]
    kv_rank, heads, nope = w_uk.shape
    q_rank = w_uq.shape[0]
    rope = w_uq.shape[1] // heads - nope
    v_dim = w_uv.shape[2]
    ssm_w = ssm_d.shape[0]
    g_heads, g_dim = gmlp_ln_g.shape
    gw = g_heads * g_dim
    chunk = gmlp_w_s.shape[1]
    o0, o1, o2, o3 = q_rank, q_rank + kv_rank, q_rank + kv_rank + rope, q_rank + kv_rank + rope + ssm_w
    w_kr = w_in[:, o1:o2]
    uq = w_uq.reshape(q_rank, heads, nope + rope)
    uq_r = uq[:, :, nope:]
    a, wbu, wc = _s5_discretise(ssm_log_dt, ssm_a_re, ssm_a_im, ssm_b_re, ssm_b_im, ssm_c_re, ssm_c_im)
    tril = jnp.tril(jnp.ones((chunk, chunk), bool))
    mix_p = jnp.where(tril, gmlp_w_s, 0.0)
    tril_s = jnp.tril(jnp.ones((s_new, s_new), bool))
    small = jnp.where(tril_s, gmlp_w_s[:, :s_new, :s_new], 0.0)
    reps = chunk // s_new
    mix_s = jnp.einsum("ab,hts->hatbs", jnp.eye(reps, dtype=F32), small).reshape(g_heads, chunk, chunk)
    bias_p = jnp.broadcast_to(gmlp_b_s.T[:, :, None], (chunk, g_heads, g_dim)).reshape(chunk, gw)
    bias_s = jnp.tile(jnp.broadcast_to(gmlp_b_s[:, :s_new].T[:, :, None], (s_new, g_heads, g_dim)).reshape(s_new, gw),
                      (reps, 1))
    attn_w = heads * v_dim
    n_e = router_w.shape[1]
    return {
        "wq": w_in[:, :o0].astype(BF16), "wkv": w_in[:, o0:o1].astype(BF16),
        "wkr": jnp.concatenate([w_kr, _swap_halves(w_kr)], axis=1).astype(BF16),
        "wu": w_in[:, o2:o3].astype(BF16), "wz": w_in[:, o3:].astype(BF16),
        "gq": q_norm_g[None, :], "gkv": kv_norm_g[None, :],
        "lng": gmlp_ln_g.reshape(1, gw), "lnb": gmlp_ln_b.reshape(1, gw),
        "mix": jnp.stack([mix_p, mix_s]).astype(BF16), "mixb": jnp.stack([bias_p, bias_s]),
        "wqn": jnp.transpose(uq[:, :, :nope], (1, 0, 2)).astype(BF16),
        "wqr": jnp.transpose(jnp.concatenate([uq_r, _swap_halves(uq_r)], axis=2), (1, 0, 2)).astype(BF16),
        "wuk": jnp.transpose(w_uk, (1, 2, 0)).astype(BF16),
        "wuv": jnp.transpose(w_uv, (1, 0, 2)).astype(BF16),
        "a": a, "wbu": wbu, "wc": wc, "d": ssm_d[None, :], "wglu": ssm_w_glu.astype(BF16), "bglu": ssm_b_glu[None, :],
        "woa": w_o[:attn_w].astype(BF16), "wos": w_o[attn_w:attn_w + ssm_w].astype(BF16),
        "wog": w_o[attn_w + ssm_w:].astype(BF16),
        "ln1g": ln1_g[None, :], "ln1b": ln1_b[None, :],
        "rwt": router_w.T, "rbias": jnp.broadcast_to(router_bias[:, None], (n_e, ROW_TILE)),
        "sg": sh_w_gate.astype(BF16), "su": sh_w_up.astype(BF16), "sd": sh_w_down.astype(BF16),
        "ln2g": ln2_g[None, :], "ln2b": ln2_b[None, :],
    }


def _rope_tables(pos, rope):
    half = rope // 2
    inv_freq = ROPE_THETA ** (-jnp.arange(half, dtype=F32) / half)
    ang = pos.astype(F32)[:, None] * inv_freq[None, :]
    cos, sin = jnp.cos(ang), jnp.sin(ang)
    return {"cos": jnp.concatenate([cos, cos], axis=1), "sin": jnp.concatenate([-sin, sin], axis=1)}


def _block_table(counts, n_rows):
    bm = MOE_ROWS
    padded = (counts + bm - 1) // bm * bm
    ends = jnp.cumsum(padded)
    n_blocks = ends[-1] // bm
    blocks = jnp.arange(n_rows // bm, dtype=I32)
    first_row = jnp.minimum(blocks, n_blocks - 1) * bm
    be = jnp.sum((ends[None, :] <= first_row[:, None]).astype(I32), axis=1)
    n_e = counts.shape[0]
    experts = jnp.arange(n_e, dtype=I32)
    later = (experts[None, :] > experts[:, None]) & (counts[None, :] > 0)
    next_e = jnp.min(jnp.where(later, experts[None, :], n_e), axis=1)
    next_e = jnp.where(next_e < n_e, next_e, -1).astype(I32)
    return ends - padded, jnp.minimum(be, n_e - 1), n_blocks.reshape(1).astype(I32), next_e


def _row_slots(idx, rank, offs):
    experts = jnp.arange(offs.shape[0], dtype=I32)
    return rank + jnp.sum(jnp.where(idx[..., None] == experts, offs, 0), axis=-1)


def kernel(x_prompt, x_sample, cache_kv, cache_kr, state_ssm_re, state_ssm_im, page_table, w_in, q_norm_g, w_uq, kv_norm_g, w_uk, w_uv, ssm_log_dt, ssm_a_re, ssm_a_im, ssm_b_re, ssm_b_im, ssm_c_re, ssm_c_im, ssm_d, ssm_w_glu, ssm_b_glu, gmlp_ln_g, gmlp_ln_b, gmlp_w_s, gmlp_b_s, w_o, ln1_g, ln1_b, router_w, router_bias, exp_w_gate, exp_w_up, exp_w_down, sh_w_gate, sh_w_up, sh_w_down, ln2_g, ln2_b):
    depth = w_in.shape[0]
    batch, seq, d = x_prompt.shape
    dec_b, s_new, _ = x_sample.shape
    n_pages, page = page_table.shape[1], cache_kv.shape[2]
    past = n_pages * page
    tp, ts = batch * seq, dec_b * s_new
    t = tp + ts
    kv_rank, heads, nope = w_uk.shape[1:]
    rope = cache_kr.shape[3]
    n_g, n_p = ssm_a_re.shape[1:]
    n_e = router_w.shape[2]
    assert tp % ROW_TILE == 0 and ts % ROW_TILE == 0 and seq % ROW_TILE == 0 and ROW_TILE % s_new == 0
    assert seq % KV_TILE == 0 and KV_TILE % Q_TILE == 0 and t % MOE_TOKENS == 0 and heads % Q_CHUNK_HEADS == 0
    assert n_pages % PAGE_GROUP == 0 and KEY_CHUNK % page == 0 and (PAGE_GROUP * page) % KEY_CHUNK == 0
    assert dec_b * (n_pages // PAGE_GROUP) >= PAGE_SLOTS
    assert kv_rank + rope <= QK_PAD and s_new <= LANES_V7X
    alpha = (2.0 * depth) ** 0.25
    qscale = (nope + rope) ** -0.5 * math.log2(math.e)
    n_rows = t * TOP_K + n_e * MOE_ROWS
    cache_krt = jnp.swapaxes(cache_kr, 2, 3)

    small = (w_in, q_norm_g, w_uq, kv_norm_g, w_uk, w_uv, ssm_log_dt, ssm_a_re, ssm_a_im, ssm_b_re, ssm_b_im,
             ssm_c_re, ssm_c_im, ssm_d, ssm_w_glu, ssm_b_glu, gmlp_ln_g, gmlp_ln_b, gmlp_w_s, gmlp_b_s, w_o, ln1_g,
             ln1_b, router_w, router_bias, sh_w_gate, sh_w_up, sh_w_down, ln2_g, ln2_b)
    pos = jnp.concatenate([jnp.tile(jnp.arange(seq, dtype=I32), batch),
                           jnp.tile(past + jnp.arange(s_new, dtype=I32), dec_b)])
    tabs = _rope_tables(pos, rope)
    x_p, x_s = x_prompt.reshape(tp, d), x_sample.reshape(ts, d)
    h0_p = jnp.zeros((batch, 1, 2 * n_g * n_p), F32)
    row_sub = d // 2 // LANES_V7X
    xs = jnp.zeros((n_rows * row_sub, LANES_V7X), U32)
    per_layer = []
    for l in range(depth):
        lw = _layer_weights(l, small, s_new)
        cq, c, kr, kcat, u, gm, v = _in_proj(x_p, x_s, lw, tabs)
        q_p = _mla_q(cq, lw, tabs, 0, tp, BF16, qscale)
        q_s = _mla_q(cq, lw, tabs, tp, ts, F32, qscale)
        attn_p = _attn_prompt(q_p, kcat, lw["wuv"], batch, seq)
        attn_s = _attn_sample(q_s, c, kr, tp, lw["wuv"], cache_kv, cache_krt, page_table, l)
        h0_s = jnp.concatenate([state_ssm_re[l].reshape(dec_b, 1, -1), state_ssm_im[l].reshape(dec_b, 1, -1)], axis=2)
        ssm_p, hT_p = _s5(u, h0_p, lw, 0, tp, seq)
        ssm_s, hT_s = _s5(u, h0_s, lw, tp, ts, s_new)
        x1, xp = _out_proj(x_p, x_s, attn_p, attn_s, ssm_p, ssm_s, gm, lw, alpha)
        idx, gates, rank, cnt = _router(x1, lw)
        offs, block_e, n_blocks, next_e = _block_table(cnt[:, 0].astype(I32), n_rows)
        dest = _row_slots(idx, rank, offs)
        xs = _dispatch(xp, dest, xs, row_sub)
        yb = _experts(xs, block_e, n_blocks, next_e, exp_w_gate, exp_w_up, exp_w_down, l, row_sub)
        x_p, x_s = _combine(x1, gates, dest, yb, lw, alpha, row_sub, tp)
        per_layer.append((c, kr, hT_p, hT_s, v))

    half = n_g * n_p
    st = lambda f: jnp.stack([f(p) for p in per_layer], axis=0)
    return (
        x_p.reshape(batch, seq, d),
        x_s.reshape(dec_b, s_new, d),
        st(lambda p: p[0][:tp].reshape(batch, seq, kv_rank)),
        st(lambda p: p[1][:tp].reshape(batch, seq, rope)),
        st(lambda p: p[2][:, 0, :half].reshape(batch, n_g, n_p)),
        st(lambda p: p[2][:, 0, half:].reshape(batch, n_g, n_p)),
        st(lambda p: p[0][tp:].reshape(dec_b, s_new, kv_rank)),
        st(lambda p: p[1][tp:].reshape(dec_b, s_new, rope)),
        st(lambda p: p[3][:, 0, :half].reshape(dec_b, n_g, n_p)),
        st(lambda p: p[3][:, 0, half:].reshape(dec_b, n_g, n_p)),
        st(lambda p: p[4][tp:].reshape(dec_b, s_new, -1)),
    )
```

```python
import functools
import math

import jax
import jax.numpy as jnp
from jax import lax
from jax.experimental import pallas as pl
from jax.experimental.pallas import tpu as pltpu

F32 = jnp.float32
BF16 = jnp.bfloat16
I32 = jnp.int32
U32 = jnp.uint32

LANES_V7X = 128
VMEM_LIMIT_V7X = 56 * 1024 * 1024

ROPE_THETA = 10000.0
EPS = 1e-6
TOP_K = 8
N_EXPERT_GROUPS = 8
TOPK_GROUPS = 4
ROUTE_SCALE = 2.5

ROW_TILE = 256
MOE_ROWS = 256
MOE_TOKENS = 128
Q_TILE = 128
KV_TILE = 256
Q_CHUNK_HEADS = 2
PAGE_GROUP = 16
PAGE_SLOTS = 6
KEY_CHUNK = 1024
QK_PAD = 384


def _params(*sem):
    return pltpu.CompilerParams(dimension_semantics=sem, vmem_limit_bytes=VMEM_LIMIT_V7X)


def _gelu(x):
    c = math.sqrt(2.0 / math.pi)
    return x * (0.5 * (1.0 + jnp.tanh(c * (x + 0.044715 * (x * x * x)))))


def _rms(x, g):
    return x * lax.rsqrt(jnp.mean(x * x, axis=-1, keepdims=True) + EPS) * g


def _layer_norm(x, g, b):
    mu = jnp.mean(x, axis=-1, keepdims=True)
    xc = x - mu
    var = jnp.mean(xc * xc, axis=-1, keepdims=True)
    return xc * lax.rsqrt(var + EPS) * g + b


def _dot(a, b):
    return jnp.dot(a, b, preferred_element_type=F32)


def _dot_nt(a, b):
    return lax.dot_general(a, b, (((1,), (1,)), ((), ())), preferred_element_type=F32)


def _in_proj_kernel(xp_ref, xs_ref, wq_ref, wkv_ref, wkr_ref, wu_ref, wz_ref, gq_ref, gkv_ref, cos_ref, sin_ref,
                    lng_ref, lnb_ref, mix_ref, mixb_ref,
                    cq_ref, c_ref, kr_ref, kcat_ref, u_ref, gm_ref, v_ref,
                    *, kv_rank, rope, g_heads, g_dim, chunk, n_prompt_tiles):
    x_ref = xp_ref
    xb = jnp.where(pl.program_id(0) < n_prompt_tiles, xp_ref[...], xs_ref[...]).astype(BF16)
    cq_ref[...] = _rms(_dot(xb, wq_ref[...]), gq_ref[...]).astype(BF16)
    c = _rms(_dot(xb, wkv_ref[...]), gkv_ref[...])
    c_ref[...] = c
    kr2 = _dot(xb, wkr_ref[...])
    kr = kr2[:, :rope] * cos_ref[...] + kr2[:, rope:] * sin_ref[...]
    kr_ref[...] = kr
    kcat_ref[:, :kv_rank] = c.astype(BF16)
    kcat_ref[:, kv_rank:kv_rank + rope] = kr.astype(BF16)
    kcat_ref[:, kv_rank + rope:] = jnp.zeros((kcat_ref.shape[0], kcat_ref.shape[1] - kv_rank - rope), BF16)
    u_ref[...] = _dot(xb, wu_ref[...])
    z = _gelu(_dot(xb, wz_ref[...]))
    width = g_heads * g_dim
    for h in range(g_heads):
        sl = slice(h * g_dim, (h + 1) * g_dim)
        vh = _layer_norm(z[:, width + h * g_dim: width + (h + 1) * g_dim], lng_ref[:, sl], lnb_ref[:, sl])
        v_ref[:, sl] = vh
        vb = vh.astype(BF16)
        for j in range(x_ref.shape[0] // chunk):
            rows = slice(j * chunk, (j + 1) * chunk)
            sv = _dot(mix_ref[h], vb[rows]) + mixb_ref[:, sl]
            gm_ref[rows, sl] = (z[rows, sl] * sv).astype(BF16)


def _trunk_rows(n_p, n_s):
    return (lambda i: (jnp.minimum(i, n_p - 1), 0)), (lambda i: (jnp.clip(i - n_p, 0, n_s - 1), 0))


def _in_proj(x_p, x_s, lw, tabs):
    d = x_p.shape[1]
    tm = ROW_TILE
    n_prompt_tiles, n_s = x_p.shape[0] // tm, x_s.shape[0] // tm
    t = x_p.shape[0] + x_s.shape[0]
    kv_rank, rope = lw["wkv"].shape[1], lw["wkr"].shape[1] // 2
    gw = lw["lng"].shape[1]
    g_heads, chunk = lw["mix"].shape[1], lw["mix"].shape[2]
    row = lambda i: (i, 0)
    full = lambda i: (0, 0)
    row_p, row_s = _trunk_rows(n_prompt_tiles, n_s)
    trunk = lambda i: (jnp.where(i >= n_prompt_tiles, 1, 0), 0, 0, 0)
    trunk3 = lambda i: (jnp.where(i >= n_prompt_tiles, 1, 0), 0, 0)
    kern = functools.partial(_in_proj_kernel, kv_rank=kv_rank, rope=rope, g_heads=g_heads, g_dim=gw // g_heads,
                             chunk=chunk, n_prompt_tiles=n_prompt_tiles)
    q_rank, ssm_w = lw["wq"].shape[1], lw["wu"].shape[1]
    return pl.pallas_call(
        kern,
        grid=(t // tm,),
        in_specs=[
            pl.BlockSpec((tm, d), row_p), pl.BlockSpec((tm, d), row_s),
            pl.BlockSpec(lw["wq"].shape, full), pl.BlockSpec(lw["wkv"].shape, full),
            pl.BlockSpec(lw["wkr"].shape, full), pl.BlockSpec(lw["wu"].shape, full),
            pl.BlockSpec(lw["wz"].shape, full),
            pl.BlockSpec(lw["gq"].shape, full), pl.BlockSpec(lw["gkv"].shape, full),
            pl.BlockSpec((tm, rope), row), pl.BlockSpec((tm, rope), row),
            pl.BlockSpec(lw["lng"].shape, full), pl.BlockSpec(lw["lnb"].shape, full),
            pl.BlockSpec((None,) + lw["mix"].shape[1:], trunk),
            pl.BlockSpec((None,) + lw["mixb"].shape[1:], trunk3),
        ],
        out_specs=[
            pl.BlockSpec((tm, q_rank), row), pl.BlockSpec((tm, kv_rank), row), pl.BlockSpec((tm, rope), row),
            pl.BlockSpec((tm, QK_PAD), row), pl.BlockSpec((tm, ssm_w), row), pl.BlockSpec((tm, gw), row),
            pl.BlockSpec((tm, gw), row),
        ],
        out_shape=[
            jax.ShapeDtypeStruct((t, q_rank), BF16), jax.ShapeDtypeStruct((t, kv_rank), F32),
            jax.ShapeDtypeStruct((t, rope), F32), jax.ShapeDtypeStruct((t, QK_PAD), BF16),
            jax.ShapeDtypeStruct((t, ssm_w), F32), jax.ShapeDtypeStruct((t, gw), BF16),
            jax.ShapeDtypeStruct((t, gw), F32),
        ],
        compiler_params=_params("parallel"),
        name="in_proj",
    )(x_p, x_s, lw["wq"], lw["wkv"], lw["wkr"], lw["wu"], lw["wz"], lw["gq"], lw["gkv"], tabs["cos"], tabs["sin"],
      lw["lng"], lw["lnb"], lw["mix"], lw["mixb"])


def _mla_q_kernel(cq_ref, wqn_ref, wqr_ref, wuk_ref, cos_ref, sin_ref, q_ref, *, kv_rank, rope, qscale):
    cq = cq_ref[...]
    for h in range(q_ref.shape[0]):
        qn = _dot(cq, wqn_ref[h]).astype(BF16)
        q_ref[h, :, :kv_rank] = (_dot(qn, wuk_ref[h]) * qscale).astype(q_ref.dtype)
        qr2 = _dot(cq, wqr_ref[h])
        qr = qr2[:, :rope] * cos_ref[...] + qr2[:, rope:] * sin_ref[...]
        q_ref[h, :, kv_rank:kv_rank + rope] = (qr * qscale).astype(q_ref.dtype)
        q_ref[h, :, kv_rank + rope:] = jnp.zeros((q_ref.shape[1], q_ref.shape[2] - kv_rank - rope), q_ref.dtype)


def _mla_q(cq, lw, tabs, row0, n_rows, out_dtype, qscale):
    tm = ROW_TILE
    heads, q_rank, _ = lw["wqn"].shape
    kv_rank, rope = lw["wuk"].shape[2], lw["wqr"].shape[2] // 2
    off = row0 // tm
    row = lambda i: (i + off, 0)
    full = lambda i: (0, 0, 0)
    return pl.pallas_call(
        functools.partial(_mla_q_kernel, kv_rank=kv_rank, rope=rope, qscale=qscale),
        grid=(n_rows // tm,),
        in_specs=[
            pl.BlockSpec((tm, q_rank), row),
            pl.BlockSpec(lw["wqn"].shape, full), pl.BlockSpec(lw["wqr"].shape, full),
            pl.BlockSpec(lw["wuk"].shape, full),
            pl.BlockSpec((tm, rope), row), pl.BlockSpec((tm, rope), row),
        ],
        out_specs=pl.BlockSpec((heads, tm, QK_PAD), lambda i: (0, i, 0)),
        out_shape=jax.ShapeDtypeStruct((heads, n_rows, QK_PAD), out_dtype),
        compiler_params=_params("parallel"),
        name="mla_q",
    )(cq, lw["wqn"], lw["wqr"], lw["wuk"], tabs["cos"], tabs["sin"])


def _lane_tiles(x, op):
    tiles = [x[:, i * LANES_V7X:(i + 1) * LANES_V7X] for i in range(x.shape[1] // LANES_V7X)]
    return functools.reduce(op, tiles)


def _widen(x, width):
    return jnp.tile(x, (1, width // LANES_V7X))


def _attn_prompt_kernel(q_ref, k_ref, wuv_ref, o_ref, s_scr, m_scr, l_scr, acc_scr, *, tq, tk, kv_rank, hpc):
    qi = pl.program_id(1)
    heads = q_ref.shape[0]
    cr = hpc * tq
    q = q_ref[...].reshape(heads * tq, q_ref.shape[2])
    m_scr[...] = jnp.full(m_scr.shape, -jnp.inf, F32)
    l_scr[...] = jnp.zeros(l_scr.shape, F32)
    acc_scr[...] = jnp.zeros(acc_scr.shape, F32)
    n_full = (qi * tq) // tk

    def keys(j):
        return k_ref[pl.ds(pl.multiple_of(j * tk, tk), tk), :]

    def scores(j, masked):
        s = _dot_nt(q, keys(j))
        if masked:
            qpos = qi * tq + lax.broadcasted_iota(I32, s.shape, 0) % tq
            kpos = j * tk + lax.broadcasted_iota(I32, s.shape, 1)
            s = jnp.where(kpos <= qpos, s, -jnp.inf)
        s_scr[j % 2] = s

    def weigh(j):
        v = keys(j)[:, :kv_rank]
        for c in range(heads // hpc):
            rows = slice(c * cr, (c + 1) * cr)
            s = s_scr[j % 2, rows]
            m_prev = m_scr[rows]
            m_new = jnp.maximum(m_prev, jnp.max(s, axis=-1, keepdims=True))
            alpha = jnp.exp2(m_prev - m_new)
            p = jnp.exp2(s - _widen(m_new, tk))
            l_scr[rows] = alpha * l_scr[rows] + _lane_tiles(p, jnp.add)
            acc_scr[rows] = _widen(alpha, kv_rank) * acc_scr[rows] + _dot(p.astype(BF16), v)
            m_scr[rows] = m_new

    @pl.when(n_full == 0)
    def _():
        scores(0, True)

    @pl.when(n_full > 0)
    def _():
        scores(0, False)

        def step(j, carry):
            weigh(j)
            scores(j + 1, False)
            return carry

        lax.fori_loop(0, n_full - 1, step, 0)
        weigh(n_full - 1)
        scores(n_full, True)

    weigh(n_full)
    o = (acc_scr[...] / jnp.sum(l_scr[...], axis=-1, keepdims=True)).astype(BF16)
    v_dim = wuv_ref.shape[2]
    for h in range(heads):
        o_ref[:, h * v_dim:(h + 1) * v_dim] = _dot(o[h * tq:(h + 1) * tq], wuv_ref[h])


def _attn_prompt(q, kcat, wuv, batch, seq):
    heads, tp, _ = q.shape
    kv_rank, v_dim = wuv.shape[1], wuv.shape[2]
    tq, tk = Q_TILE, KV_TILE
    nq = seq // tq
    rows = heads * tq
    return pl.pallas_call(
        functools.partial(_attn_prompt_kernel, tq=tq, tk=tk, kv_rank=kv_rank, hpc=Q_CHUNK_HEADS),
        grid=(batch, nq),
        in_specs=[
            pl.BlockSpec((heads, tq, QK_PAD), lambda b, i: (0, b * nq + i, 0)),
            pl.BlockSpec((seq, QK_PAD), lambda b, i: (b, 0)),
            pl.BlockSpec(wuv.shape, lambda b, i: (0, 0, 0)),
        ],
        out_specs=pl.BlockSpec((tq, heads * v_dim), lambda b, i: (b * nq + i, 0)),
        out_shape=jax.ShapeDtypeStruct((tp, heads * v_dim), F32),
        scratch_shapes=[pltpu.VMEM((2, rows, tk), F32), pltpu.VMEM((rows, LANES_V7X), F32),
                        pltpu.VMEM((rows, LANES_V7X), F32), pltpu.VMEM((rows, kv_rank), F32)],
        compiler_params=_params("parallel", "arbitrary"),
        name="attn_prompt",
    )(q, kcat, wuv)


def _attn_sample_kernel(pt_ref, q_ref, cnew_ref, krnew_ref, wuv_ref, ckv_hbm, ckrt_hbm, o_ref,
                        kvbuf, krbuf, kb_all, krt_scr, knew_scr, s_scr, sems,
                        *, layer, n_pages, group, page, kv_rank, rope, chunk):
    b = pl.program_id(0)
    heads, s_new, _ = q_ref.shape
    rows = heads * s_new
    n_groups = n_pages // group
    ppc = chunk // page
    cpg = group // ppc
    q = q_ref[...].reshape(rows, q_ref.shape[2]).astype(BF16)
    q_lat, q_rope = q[:, :kv_rank], q[:, kv_rank:]

    n_slots = kvbuf.shape[0]
    ahead = n_slots - 1
    total = pl.num_programs(0) * n_groups

    def copies(gg, slot, p):
        pg = pt_ref[gg * group + p]
        return (pltpu.make_async_copy(ckv_hbm.at[layer, pg], kvbuf.at[slot, p], sems.at[0, slot]),
                pltpu.make_async_copy(ckrt_hbm.at[layer, pg], krbuf.at[slot, p], sems.at[1, slot]))

    def start(gg):
        for p in range(group):
            for cp in copies(gg, gg % n_slots, p):
                cp.start(priority=p % 2)

    def wait(slot):
        for p in range(group):
            for cp in copies(0, slot, p):
                cp.wait()

    @pl.when(b == 0)
    def _():
        for gg in range(ahead):
            start(gg)

    krt_scr[:, rope:, :] = jnp.zeros((cpg, krt_scr.shape[1] - rope, chunk), BF16)

    def stream(g, m_part):
        gg = b * n_groups + g
        slot = gg % n_slots

        @pl.when(gg + ahead < total)
        def _():
            start(gg + ahead)

        wait(slot)
        for c in range(cpg):
            kvb = kvbuf[slot, c * ppc:(c + 1) * ppc].reshape(chunk, kv_rank).astype(BF16)
            kb_all[g * cpg + c] = kvb
            for p in range(ppc):
                krt_scr[c, :rope, p * page:(p + 1) * page] = krbuf[slot, c * ppc + p].astype(BF16)
            s = _dot_nt(q_lat, kvb) + _dot(q_rope, krt_scr[c])
            s_scr[g * cpg + c] = s
            m_part = jnp.maximum(m_part, _lane_tiles(s, jnp.maximum))
        return m_part

    m_part = lax.fori_loop(0, n_groups, stream, jnp.full((rows, LANES_V7X), -jnp.inf, F32))

    knew_scr[...] = jnp.zeros(knew_scr.shape, BF16)
    knew_scr[:s_new, :kv_rank] = cnew_ref[...].astype(BF16)
    knew_scr[:s_new, kv_rank:kv_rank + rope] = krnew_ref[...].astype(BF16)
    kn = knew_scr[...]
    sn = _dot_nt(q, kn)
    qpos = lax.broadcasted_iota(I32, sn.shape, 0) % s_new
    kpos = lax.broadcasted_iota(I32, sn.shape, 1)
    sn = jnp.where(kpos <= qpos, sn, -jnp.inf)
    m = jnp.max(jnp.maximum(m_part, sn), axis=-1, keepdims=True)
    m_wide = jnp.broadcast_to(m, (rows, chunk))

    def weigh(i, carry):
        l_part, acc = carry
        p = jnp.exp2(s_scr[i] - m_wide)
        return l_part + _lane_tiles(p, jnp.add), acc + _dot(p.astype(BF16), kb_all[i])

    pn = jnp.exp2(sn - m_wide[:, :LANES_V7X])
    l_part, acc = lax.fori_loop(0, n_groups * cpg, weigh, (pn, _dot(pn.astype(BF16), kn[:, :kv_rank])), unroll=2)
    o = (acc / jnp.sum(l_part, axis=-1, keepdims=True)).astype(BF16)
    v_dim = wuv_ref.shape[2]
    for h in range(heads):
        o_ref[:, h * v_dim:(h + 1) * v_dim] = _dot(o, wuv_ref[h])[h * s_new:(h + 1) * s_new]


def _attn_sample(q, c, kr, row0, wuv, cache_kv, cache_krt, page_table, layer):
    heads, ts, _ = q.shape
    dec_b, n_pages = page_table.shape
    s_new = ts // dec_b
    page, kv_rank = cache_kv.shape[2], cache_kv.shape[3]
    rope = cache_krt.shape[2]
    v_dim = wuv.shape[2]
    group, chunk = PAGE_GROUP, KEY_CHUNK
    n_chunks = n_pages * page // chunk
    rows = heads * s_new
    off = row0 // s_new
    grid_spec = pltpu.PrefetchScalarGridSpec(
        num_scalar_prefetch=1,
        grid=(dec_b,),
        in_specs=[
            pl.BlockSpec((heads, s_new, QK_PAD), lambda b, pt: (0, b, 0)),
            pl.BlockSpec((s_new, kv_rank), lambda b, pt: (b + off, 0)),
            pl.BlockSpec((s_new, rope), lambda b, pt: (b + off, 0)),
            pl.BlockSpec(wuv.shape, lambda b, pt: (0, 0, 0)),
            pl.BlockSpec(memory_space=pl.ANY),
            pl.BlockSpec(memory_space=pl.ANY),
        ],
        out_specs=pl.BlockSpec((s_new, heads * v_dim), lambda b, pt: (b, 0)),
        scratch_shapes=[
            pltpu.VMEM((PAGE_SLOTS, group, page, kv_rank), F32),
            pltpu.VMEM((PAGE_SLOTS, group, rope, page), F32),
            pltpu.VMEM((n_chunks, chunk, kv_rank), BF16),
            pltpu.VMEM((group * page // chunk, QK_PAD - kv_rank, chunk), BF16),
            pltpu.VMEM((LANES_V7X, QK_PAD), BF16),
            pltpu.VMEM((n_chunks, rows, chunk), F32),
            pltpu.SemaphoreType.DMA((2, PAGE_SLOTS)),
        ],
    )
    return pl.pallas_call(
        functools.partial(_attn_sample_kernel, layer=layer, n_pages=n_pages, group=group, page=page,
                          kv_rank=kv_rank, rope=rope, chunk=chunk),
        grid_spec=grid_spec,
        out_shape=jax.ShapeDtypeStruct((ts, heads * v_dim), F32),
        compiler_params=_params("arbitrary"),
        name="attn_sample",
    )(page_table.reshape(-1), q, c, kr, wuv, cache_kv, cache_krt)


def _s5_kernel(u_ref, h0_ref, wbu_ref, a_ref, wc_ref, d_ref, wglu_ref, bglu_ref, y_ref, ht_ref, hs_scr, carry_scr,
               *, seq_rows, seqs_per_tile, tiles_per_seq):
    i = pl.program_id(0)
    u = u_ref[...]
    hs_scr[...] = _dot(u.astype(BF16), wbu_ref[...])
    n = a_ref.shape[1]
    a_re, a_im = a_ref[0:1, :], a_ref[1:2, :]

    if tiles_per_seq > 1:
        @pl.when(i % tiles_per_seq == 0)
        def _():
            carry_scr[...] = h0_ref[0]

    def one_sequence(s, carry):
        h = carry_scr[...] if tiles_per_seq > 1 else h0_ref[s]

        def row(t, hc):
            hr, hi = hc
            r = s * seq_rows + t
            nr = a_re * hr - a_im * hi + hs_scr[pl.ds(r, 1), :n]
            ni = a_re * hi + a_im * hr + hs_scr[pl.ds(r, 1), n:]
            hs_scr[pl.ds(r, 1), :n] = nr
            hs_scr[pl.ds(r, 1), n:] = ni
            return nr, ni

        hr, hi = lax.fori_loop(0, seq_rows, row, (h[:, :n], h[:, n:]), unroll=8)
        last = jnp.concatenate([hr, hi], axis=1)
        ht_ref[s] = last
        if tiles_per_seq > 1:
            carry_scr[...] = last
        return carry

    lax.fori_loop(0, seqs_per_tile, one_sequence, 0)
    y = _dot(hs_scr[...].astype(BF16), wc_ref[...]) + d_ref[...] * u
    g = _gelu(y)
    y_ref[...] = g * jax.nn.sigmoid(_dot(g.astype(BF16), wglu_ref[...]) + bglu_ref[...])


def _s5(u, h0, lw, row0, n_rows, seq_len):
    tc = ROW_TILE
    width = u.shape[1]
    n2 = lw["wbu"].shape[1]
    seq_rows = min(seq_len, tc)
    seqs_per_tile = tc // seq_rows
    tiles_per_seq = max(1, seq_len // tc)
    n_seq = n_rows // seq_len
    off = row0 // tc
    full = lambda i: (0, 0)
    seq_idx = (lambda i: (i // tiles_per_seq, 0, 0)) if tiles_per_seq > 1 else (lambda i: (i, 0, 0))
    return pl.pallas_call(
        functools.partial(_s5_kernel, seq_rows=seq_rows, seqs_per_tile=seqs_per_tile, tiles_per_seq=tiles_per_seq),
        grid=(n_rows // tc,),
        in_specs=[
            pl.BlockSpec((tc, width), lambda i: (i + off, 0)),
            pl.BlockSpec((seqs_per_tile, 1, n2), seq_idx),
            pl.BlockSpec(lw["wbu"].shape, full), pl.BlockSpec(lw["a"].shape, full),
            pl.BlockSpec(lw["wc"].shape, full), pl.BlockSpec(lw["d"].shape, full),
            pl.BlockSpec(lw["wglu"].shape, full), pl.BlockSpec(lw["bglu"].shape, full),
        ],
        out_specs=[pl.BlockSpec((tc, width), lambda i: (i, 0)), pl.BlockSpec((seqs_per_tile, 1, n2), seq_idx)],
        out_shape=[jax.ShapeDtypeStruct((n_rows, width), F32), jax.ShapeDtypeStruct((n_seq, 1, n2), F32)],
        scratch_shapes=[pltpu.VMEM((tc, n2), F32), pltpu.VMEM((1, n2), F32)],
        compiler_params=_params("arbitrary"),
        name="s5",
    )(u, h0, lw["wbu"], lw["a"], lw["wc"], lw["d"], lw["wglu"], lw["bglu"])


def _pack_bf16_pairs(x):
    half = x.shape[1] // 2
    hi = pltpu.bitcast(x[:, :half].astype(BF16).astype(F32), U32)
    lo = pltpu.bitcast(x[:, half:].astype(BF16).astype(F32), U32)
    return hi | (lo >> 16)


def _unpack_bf16_pairs(p):
    hi = pltpu.bitcast(p & jnp.uint32(0xFFFF0000), F32)
    lo = pltpu.bitcast(p << 16, F32)
    return hi, lo


def _store_row_tiles(ref, x):
    rows, n = x.shape[0], x.shape[1] // LANES_V7X
    for j in range(n):
        ref[pl.ds(j, rows, stride=n), :] = x[:, j * LANES_V7X:(j + 1) * LANES_V7X]


def _load_row_tiles(ref, rows):
    n = ref.shape[0] // rows
    return jnp.concatenate([ref[pl.ds(j, rows, stride=n), :] for j in range(n)], axis=1)


def _out_proj_kernel(xa_ref, xb_ref, ap_ref, as_ref, sp_ref, ss_ref, g_ref, woa_ref, wos_ref, wog_ref, lg_ref, lb_ref,
                     x1_ref, xp_ref, *, alpha, n_prompt_tiles):
    prompt = pl.program_id(0) < n_prompt_tiles
    x = jnp.where(prompt, xa_ref[...], xb_ref[...])
    a = jnp.where(prompt, ap_ref[...], as_ref[...])
    s = jnp.where(prompt, sp_ref[...], ss_ref[...])
    mix = _dot(a.astype(BF16), woa_ref[...]) + _dot(s.astype(BF16), wos_ref[...]) + _dot(g_ref[...], wog_ref[...])
    x1 = _layer_norm(alpha * x + mix, lg_ref[...], lb_ref[...])
    x1_ref[...] = x1
    _store_row_tiles(xp_ref, _pack_bf16_pairs(x1))


def _out_proj(x_p, x_s, attn_p, attn_s, ssm_p, ssm_s, gm, lw, alpha):
    d = x_p.shape[1]
    t = x_p.shape[0] + x_s.shape[0]
    tm = ROW_TILE
    n_p, n_s = attn_p.shape[0] // tm, attn_s.shape[0] // tm
    row_sub = d // 2 // LANES_V7X
    row = lambda i: (i, 0)
    full = lambda i: (0, 0)
    row_p, row_s = _trunk_rows(n_p, n_s)
    return pl.pallas_call(
        functools.partial(_out_proj_kernel, alpha=alpha, n_prompt_tiles=n_p),
        grid=(t // tm,),
        in_specs=[
            pl.BlockSpec((tm, d), row_p), pl.BlockSpec((tm, d), row_s),
            pl.BlockSpec((tm, attn_p.shape[1]), row_p), pl.BlockSpec((tm, attn_s.shape[1]), row_s),
            pl.BlockSpec((tm, ssm_p.shape[1]), row_p), pl.BlockSpec((tm, ssm_s.shape[1]), row_s),
            pl.BlockSpec((tm, gm.shape[1]), row),
            pl.BlockSpec(lw["woa"].shape, full), pl.BlockSpec(lw["wos"].shape, full), pl.BlockSpec(lw["wog"].shape, full),
            pl.BlockSpec(lw["ln1g"].shape, full), pl.BlockSpec(lw["ln1b"].shape, full),
        ],
        out_specs=[pl.BlockSpec((tm, d), row), pl.BlockSpec((tm * row_sub, LANES_V7X), row)],
        out_shape=[jax.ShapeDtypeStruct((t, d), F32), jax.ShapeDtypeStruct((t * row_sub, LANES_V7X), U32)],
        compiler_params=_params("parallel"),
        name="out_proj",
    )(x_p, x_s, attn_p, attn_s, ssm_p, ssm_s, gm, lw["woa"], lw["wos"], lw["wog"], lw["ln1g"], lw["ln1b"])


def _first_index_of_max(x, iota, n):
    m = jnp.max(x, axis=0, keepdims=True)
    return m, jnp.min(jnp.where(x == m, iota, n), axis=0, keepdims=True)


def _router_kernel(x_ref, rwt_ref, bias_ref, idx_ref, gate_ref, rank_ref, cnt_ref, carry_scr):
    i = pl.program_id(0)

    @pl.when(i == 0)
    def _():
        carry_scr[...] = jnp.zeros(carry_scr.shape, F32)

    n_e = rwt_ref.shape[0]
    tm = x_ref.shape[0]
    per_group = n_e // N_EXPERT_GROUPS
    logits = lax.dot_general(rwt_ref[...], x_ref[...], (((1,), (1,)), ((), ())),
                             precision=lax.Precision.HIGHEST, preferred_element_type=F32)
    scores = jax.nn.sigmoid(logits)
    sel = scores + bias_ref[...]
    neg = -jnp.inf
    sub = lax.broadcasted_iota(I32, (per_group, tm), 0)
    group_scores = []
    for g in range(N_EXPERT_GROUPS):
        blk = sel[g * per_group:(g + 1) * per_group, :]
        m1, i1 = _first_index_of_max(blk, sub, per_group)
        m2 = jnp.max(jnp.where(sub == i1, neg, blk), axis=0, keepdims=True)
        group_scores.append(m1 + m2)
    work = jnp.concatenate(group_scores, axis=0)
    g_iota = lax.broadcasted_iota(I32, work.shape, 0)
    keep = jnp.zeros(work.shape, F32)
    for _ in range(TOPK_GROUPS):
        _, gi = _first_index_of_max(work, g_iota, N_EXPERT_GROUPS)
        hit = g_iota == gi
        keep = jnp.where(hit, 1.0, keep)
        work = jnp.where(hit, neg, work)
    keep_e = jnp.concatenate([jnp.broadcast_to(keep[g:g + 1, :], (per_group, tm)) for g in range(N_EXPERT_GROUPS)], axis=0)
    cand = jnp.where(keep_e > 0.0, sel, neg)
    e_iota = lax.broadcasted_iota(I32, (n_e, tm), 0)
    onehot = jnp.zeros((n_e, tm), F32)
    idx_rows, gate_rows = [], []
    for _ in range(TOP_K):
        _, ik = _first_index_of_max(cand, e_iota, n_e)
        hit = e_iota == ik
        gate_rows.append(jnp.sum(jnp.where(hit, scores, 0.0), axis=0, keepdims=True))
        onehot = jnp.where(hit, 1.0, onehot)
        cand = jnp.where(hit, neg, cand)
        idx_rows.append(ik)
    gates = jnp.concatenate(gate_rows, axis=0)
    gate_ref[...] = gates / jnp.sum(gates, axis=0, keepdims=True) * ROUTE_SCALE
    idx_ref[...] = jnp.concatenate(idx_rows, axis=0)
    before = (lax.broadcasted_iota(I32, (tm, tm), 0) < lax.broadcasted_iota(I32, (tm, tm), 1)).astype(BF16)
    prefix = _dot(onehot.astype(BF16), before) + carry_scr[:, :1]
    rank_rows = [jnp.sum(jnp.where(e_iota == ik, prefix, 0.0), axis=0, keepdims=True) for ik in idx_rows]
    rank_ref[...] = jnp.concatenate(rank_rows, axis=0).astype(I32)
    carry_scr[...] = carry_scr[...] + jnp.sum(onehot, axis=1, keepdims=True)
    cnt_ref[...] = carry_scr[...]


def _router(x1, lw):
    t, d = x1.shape
    tm = ROW_TILE
    n_e = lw["rwt"].shape[0]
    col = lambda i: (0, i)
    full = lambda i: (0, 0)
    return pl.pallas_call(
        _router_kernel,
        grid=(t // tm,),
        in_specs=[pl.BlockSpec((tm, d), lambda i: (i, 0)), pl.BlockSpec(lw["rwt"].shape, full),
                  pl.BlockSpec(lw["rbias"].shape, full)],
        out_specs=[pl.BlockSpec((TOP_K, tm), col), pl.BlockSpec((TOP_K, tm), col), pl.BlockSpec((TOP_K, tm), col),
                   pl.BlockSpec((n_e, LANES_V7X), full)],
        out_shape=[jax.ShapeDtypeStruct((TOP_K, t), I32), jax.ShapeDtypeStruct((TOP_K, t), F32),
                   jax.ShapeDtypeStruct((TOP_K, t), I32), jax.ShapeDtypeStruct((n_e, LANES_V7X), F32)],
        scratch_shapes=[pltpu.VMEM((n_e, LANES_V7X), F32)],
        compiler_params=_params("arbitrary"),
        name="router",
    )(x1, lw["rwt"], lw["rbias"])


def _dispatch_kernel(dest_ref, xp_ref, xs_in, xs_hbm, sem, *, row_sub):
    del xs_in
    tokens = xp_ref.shape[0] // row_sub

    def copy(tok, k):
        src = pl.ds(pl.multiple_of(tok * row_sub, row_sub), row_sub)
        dst = pl.ds(pl.multiple_of(dest_ref[k, tok] * row_sub, row_sub), row_sub)
        return pltpu.make_async_copy(xp_ref.at[src, :], xs_hbm.at[dst, :], sem)

    def start(tok, carry):
        for k in range(TOP_K):
            copy(tok, k).start()
        return carry

    def wait(tok, carry):
        for k in range(TOP_K):
            copy(tok, k).wait()
        return carry

    lax.fori_loop(0, tokens, start, 0)
    lax.fori_loop(0, tokens, wait, 0)


def _dispatch(xp, dest, xs_buf, row_sub):
    tm = MOE_TOKENS
    t = xp.shape[0] // row_sub
    return pl.pallas_call(
        functools.partial(_dispatch_kernel, row_sub=row_sub),
        grid=(t // tm,),
        in_specs=[pl.BlockSpec((TOP_K, tm), lambda i: (0, i), memory_space=pltpu.SMEM),
                  pl.BlockSpec((tm * row_sub, LANES_V7X), lambda i: (i, 0)),
                  pl.BlockSpec(memory_space=pl.ANY)],
        out_specs=pl.BlockSpec(memory_space=pl.ANY),
        out_shape=jax.ShapeDtypeStruct(xs_buf.shape, U32),
        scratch_shapes=[pltpu.SemaphoreType.DMA(())],
        input_output_aliases={2: 0},
        compiler_params=_params("arbitrary"),
        name="moe_dispatch",
    )(dest, xp, xs_buf)


def _experts_kernel(be_ref, nb_ref, nxt_ref, xs_ref, wg_hbm, wu_hbm, wd_hbm, y_ref, wg_f, wu_f, wd_f, wg_b, wu_b, wd_b,
                    sems, *, layer, rows):
    i = pl.program_id(0)
    e = be_ref[i]
    fresh = jnp.logical_or(i == 0, e != be_ref[jnp.maximum(i - 1, 0)])

    def copies(ex):
        return (pltpu.make_async_copy(wg_hbm.at[layer, ex], wg_f, sems.at[0]),
                pltpu.make_async_copy(wu_hbm.at[layer, ex], wu_f, sems.at[1]),
                pltpu.make_async_copy(wd_hbm.at[layer, ex], wd_f, sems.at[2]))

    @pl.when(i == 0)
    def _():
        for cp in copies(e):
            cp.start()

    @pl.when(fresh)
    def _():
        for cp in copies(e):
            cp.wait()
        wg_b[...] = wg_f[...].astype(BF16)
        wu_b[...] = wu_f[...].astype(BF16)
        wd_b[...] = wd_f[...].astype(BF16)

        @pl.when(nxt_ref[e] >= 0)
        def _():
            for cp in copies(nxt_ref[e]):
                cp.start(priority=1)

    @pl.when(i < nb_ref[0])
    def _():
        hi, lo = _unpack_bf16_pairs(_load_row_tiles(xs_ref, rows))
        hi, lo = hi.astype(BF16), lo.astype(BF16)
        half = hi.shape[1]
        g = _dot(hi, wg_b[:half, :]) + _dot(lo, wg_b[half:, :])
        u = _dot(hi, wu_b[:half, :]) + _dot(lo, wu_b[half:, :])
        y = _dot((jax.nn.silu(g) * u).astype(BF16), wd_b[...])
        _store_row_tiles(y_ref, _pack_bf16_pairs(y))

    @pl.when(i >= nb_ref[0])
    def _():
        y_ref[...] = jnp.zeros(y_ref.shape, U32)


def _experts(xs, block_e, n_blocks, next_e, wg, wu, wd, layer, row_sub):
    bm = MOE_ROWS
    n_rows = xs.shape[0] // row_sub
    d, ff = wg.shape[2], wg.shape[3]
    blk = (bm * row_sub, LANES_V7X)
    grid_spec = pltpu.PrefetchScalarGridSpec(
        num_scalar_prefetch=3,
        grid=(n_rows // bm,),
        in_specs=[
            pl.BlockSpec(blk, lambda i, be, nb, nx: (jnp.minimum(i, nb[0] - 1), 0)),
            pl.BlockSpec(memory_space=pl.ANY), pl.BlockSpec(memory_space=pl.ANY), pl.BlockSpec(memory_space=pl.ANY),
        ],
        out_specs=pl.BlockSpec(blk, lambda i, be, nb, nx: (i, 0)),
        scratch_shapes=[pltpu.VMEM((d, ff), F32), pltpu.VMEM((d, ff), F32), pltpu.VMEM((ff, d), F32),
                        pltpu.VMEM((d, ff), BF16), pltpu.VMEM((d, ff), BF16), pltpu.VMEM((ff, d), BF16),
                        pltpu.SemaphoreType.DMA((3,))],
    )
    return pl.pallas_call(
        functools.partial(_experts_kernel, layer=layer, rows=bm),
        grid_spec=grid_spec,
        out_shape=jax.ShapeDtypeStruct(xs.shape, U32),
        compiler_params=_params("arbitrary"),
        name="moe_experts",
    )(block_e, n_blocks, next_e, xs, wg, wu, wd)


def _combine_kernel(dest_ref, dnext_ref, x1_ref, gate_ref, sg_ref, su_ref, sd_ref, lg_ref, lb_ref, yb_hbm,
                    x2p_ref, x2s_ref, buf, sems, *, alpha, row_sub, n_prompt_tiles):
    i = pl.program_id(0)
    tokens = x1_ref.shape[0]
    slot = i % 2

    def copy(dref, s, tok, k):
        src = pl.ds(pl.multiple_of(dref[k, tok] * row_sub, row_sub), row_sub)
        dst = pl.ds(pl.multiple_of(tok * row_sub, row_sub), row_sub)
        return pltpu.make_async_copy(yb_hbm.at[src, :], buf.at[s, k, dst, :], sems.at[s])

    def gather(dref, s):
        def start(tok, carry):
            for k in range(TOP_K):
                copy(dref, s, tok, k).start()
            return carry

        lax.fori_loop(0, tokens, start, 0)

    def wait(tok, carry):
        for k in range(TOP_K):
            copy(dest_ref, slot, tok, k).wait()
        return carry

    @pl.when(i == 0)
    def _():
        gather(dest_ref, 0)

    @pl.when(i + 1 < pl.num_programs(0))
    def _():
        gather(dnext_ref, 1 - slot)

    x1 = x1_ref[...]
    xb = x1.astype(BF16)
    h = (jax.nn.silu(_dot(xb, sg_ref[...])) * _dot(xb, su_ref[...])).astype(BF16)
    acc = alpha * x1 + _dot(h, sd_ref[...])
    gates = jnp.concatenate([gate_ref[...], jnp.zeros((tokens - TOP_K, tokens), F32)], axis=0).T
    lax.fori_loop(0, tokens, wait, 0)
    half = x1.shape[1] // 2
    routed_hi = jnp.zeros((tokens, half), F32)
    routed_lo = jnp.zeros((tokens, half), F32)
    for k in range(TOP_K):
        hi, lo = _unpack_bf16_pairs(_load_row_tiles(buf.at[slot, k], tokens))
        routed_hi = routed_hi + gates[:, k:k + 1] * hi
        routed_lo = routed_lo + gates[:, k:k + 1] * lo
    acc = acc + jnp.concatenate([routed_hi, routed_lo], axis=1)
    x2 = _layer_norm(acc, lg_ref[...], lb_ref[...])

    @pl.when(i < n_prompt_tiles)
    def _():
        x2p_ref[...] = x2

    @pl.when(i >= n_prompt_tiles)
    def _():
        x2s_ref[...] = x2


def _combine(x1, gates, dest, yb, lw, alpha, row_sub, tp):
    t, d = x1.shape
    tm = MOE_TOKENS
    n = t // tm
    n_p = tp // tm
    full = lambda i: (0, 0)
    row_p, row_s = _trunk_rows(n_p, n - n_p)
    return pl.pallas_call(
        functools.partial(_combine_kernel, alpha=alpha, row_sub=row_sub, n_prompt_tiles=n_p),
        grid=(n,),
        in_specs=[
            pl.BlockSpec((TOP_K, tm), lambda i: (0, i), memory_space=pltpu.SMEM),
            pl.BlockSpec((TOP_K, tm), lambda i: (0, jnp.minimum(i + 1, n - 1)), memory_space=pltpu.SMEM),
            pl.BlockSpec((tm, d), lambda i: (i, 0)),
            pl.BlockSpec((TOP_K, tm), lambda i: (0, i)),
            pl.BlockSpec(lw["sg"].shape, full), pl.BlockSpec(lw["su"].shape, full), pl.BlockSpec(lw["sd"].shape, full),
            pl.BlockSpec(lw["ln2g"].shape, full), pl.BlockSpec(lw["ln2b"].shape, full),
            pl.BlockSpec(memory_space=pl.ANY),
        ],
        out_specs=[pl.BlockSpec((tm, d), row_p), pl.BlockSpec((tm, d), row_s)],
        out_shape=[jax.ShapeDtypeStruct((tp, d), F32), jax.ShapeDtypeStruct((t - tp, d), F32)],
        scratch_shapes=[pltpu.VMEM((2, TOP_K, tm * row_sub, LANES_V7X), U32), pltpu.SemaphoreType.DMA((2,))],
        compiler_params=_params("arbitrary"),
        name="moe_combine",
    )(dest, dest, x1, gates, lw["sg"], lw["su"], lw["sd"], lw["ln2g"], lw["ln2b"], yb)


def _swap_halves(w):
    half = w.shape[-1] // 2
    return jnp.concatenate([w[..., half:], w[..., :half]], axis=-1)


def _s5_discretise(log_dt, a_re, a_im, b_re, b_im, c_re, c_im):
    n_g, n_p = a_re.shape
    n_h = b_re.shape[2]
    dt = jnp.exp(log_dt)[:, None]
    mag = jnp.exp(a_re * dt)
    ab_re, ab_im = mag * jnp.cos(a_im * dt), mag * jnp.sin(a_im * dt)
    den = jnp.square(a_re) + jnp.square(a_im)
    f_re = ((ab_re - 1.0) * a_re + ab_im * a_im) / den
    f_im = (ab_im * a_re - (ab_re - 1.0) * a_im) / den
    bb_re = f_re[..., None] * b_re - f_im[..., None] * b_im
    bb_im = f_re[..., None] * b_im + f_im[..., None] * b_re
    eye = jnp.eye(n_g, dtype=F32)
    blk = lambda m: jnp.einsum("gph,gk->ghkp", m, eye).reshape(n_g * n_h, n_g * n_p)
    wbu = jnp.concatenate([blk(bb_re), blk(bb_im)], axis=1)
    blk_c = lambda m: jnp.einsum("ghp,gk->kpgh", m, eye).reshape(n_g * n_p, n_g * n_h)
    wc = jnp.concatenate([blk_c(c_re), blk_c(-c_im)], axis=0)
    a = jnp.stack([ab_re.reshape(-1), ab_im.reshape(-1)], axis=0)
    return a, wbu.astype(BF16), wc.astype(BF16)


def _layer_weights(l, w, s_new):
    (w_in, q_norm_g, w_uq, kv_norm_g, w_uk, w_uv, ssm_log_dt, ssm_a_re, ssm_a_im, ssm_b_re, ssm_b_im, ssm_c_re,
     ssm_c_im, ssm_d, ssm_w_glu, ssm_b_glu, gmlp_ln_g, gmlp_ln_b, gmlp_w_s, gmlp_b_s, w_o, ln1_g, ln1_b, router_w,
     router_bias, sh_w_gate, sh_w_up, sh_w_down, ln2_g, ln2_b) = [a[l] for a in w]
    kv_rank, heads, nope = w_uk.shape
    q_rank = w_uq.shape[0]
    rope = w_uq.shape[1] // heads - nope
    v_dim = w_uv.shape[2]
    ssm_w = ssm_d.shape[0]
    g_heads, g_dim = gmlp_ln_g.shape
    gw = g_heads * g_dim
    chunk = gmlp_w_s.shape[1]
    o0, o1, o2, o3 = q_rank, q_rank + kv_rank, q_rank + kv_rank + rope, q_rank + kv_rank + rope + ssm_w
    w_kr = w_in[:, o1:o2]
    uq = w_uq.reshape(q_rank, heads, nope + rope)
    uq_r = uq[:, :, nope:]
    a, wbu, wc = _s5_discretise(ssm_log_dt, ssm_a_re, ssm_a_im, ssm_b_re, ssm_b_im, ssm_c_re, ssm_c_im)
    tril = jnp.tril(jnp.ones((chunk, chunk), bool))
    mix_p = jnp.where(tril, gmlp_w_s, 0.0)
    tril_s = jnp.tril(jnp.ones((s_new, s_new), bool))
    small = jnp.where(tril_s, gmlp_w_s[:, :s_new, :s_new], 0.0)
    reps = chunk // s_new
    mix_s = jnp.einsum("ab,hts->hatbs", jnp.eye(reps, dtype=F32), small).reshape(g_heads, chunk, chunk)
    bias_p = jnp.broadcast_to(gmlp_b_s.T[:, :, None], (chunk, g_heads, g_dim)).reshape(chunk, gw)
    bias_s = jnp.tile(jnp.broadcast_to(gmlp_b_s[:, :s_new].T[:, :, None], (s_new, g_heads, g_dim)).reshape(s_new, gw),
                      (reps, 1))
    attn_w = heads * v_dim
    n_e = router_w.shape[1]
    return {
        "wq": w_in[:, :o0].astype(BF16), "wkv": w_in[:, o0:o1].astype(BF16),
        "wkr": jnp.concatenate([w_kr, _swap_halves(w_kr)], axis=1).astype(BF16),
        "wu": w_in[:, o2:o3].astype(BF16), "wz": w_in[:, o3:].astype(BF16),
        "gq": q_norm_g[None, :], "gkv": kv_norm_g[None, :],
        "lng": gmlp_ln_g.reshape(1, gw), "lnb": gmlp_ln_b.reshape(1, gw),
        "mix": jnp.stack([mix_p, mix_s]).astype(BF16), "mixb": jnp.stack([bias_p, bias_s]),
        "wqn": jnp.transpose(uq[:, :, :nope], (1, 0, 2)).astype(BF16),
        "wqr": jnp.transpose(jnp.concatenate([uq_r, _swap_halves(uq_r)], axis=2), (1, 0, 2)).astype(BF16),
        "wuk": jnp.transpose(w_uk, (1, 2, 0)).astype(BF16),
        "wuv": jnp.transpose(w_uv, (1, 0, 2)).astype(BF16),
        "a": a, "wbu": wbu, "wc": wc, "d": ssm_d[None, :], "wglu": ssm_w_glu.astype(BF16), "bglu": ssm_b_glu[None, :],
        "woa": w_o[:attn_w].astype(BF16), "wos": w_o[attn_w:attn_w + ssm_w].astype(BF16),
        "wog": w_o[attn_w + ssm_w:].astype(BF16),
        "ln1g": ln1_g[None, :], "ln1b": ln1_b[None, :],
        "rwt": router_w.T, "rbias": jnp.broadcast_to(router_bias[:, None], (n_e, ROW_TILE)),
        "sg": sh_w_gate.astype(BF16), "su": sh_w_up.astype(BF16), "sd": sh_w_down.astype(BF16),
        "ln2g": ln2_g[None, :], "ln2b": ln2_b[None, :],
    }


def _rope_tables(pos, rope):
    half = rope // 2
    inv_freq = ROPE_THETA ** (-jnp.arange(half, dtype=F32) / half)
    ang = pos.astype(F32)[:, None] * inv_freq[None, :]
    cos, sin = jnp.cos(ang), jnp.sin(ang)
    return {"cos": jnp.concatenate([cos, cos], axis=1), "sin": jnp.concatenate([-sin, sin], axis=1)}


def _block_table(counts, n_rows):
    bm = MOE_ROWS
    padded = (counts + bm - 1) // bm * bm
    ends = jnp.cumsum(padded)
    n_blocks = ends[-1] // bm
    blocks = jnp.arange(n_rows // bm, dtype=I32)
    first_row = jnp.minimum(blocks, n_blocks - 1) * bm
    be = jnp.sum((ends[None, :] <= first_row[:, None]).astype(I32), axis=1)
    n_e = counts.shape[0]
    experts = jnp.arange(n_e, dtype=I32)
    later = (experts[None, :] > experts[:, None]) & (counts[None, :] > 0)
    next_e = jnp.min(jnp.where(later, experts[None, :], n_e), axis=1)
    next_e = jnp.where(next_e < n_e, next_e, -1).astype(I32)
    return ends - padded, jnp.minimum(be, n_e - 1), n_blocks.reshape(1).astype(I32), next_e


def _row_slots(idx, rank, offs):
    experts = jnp.arange(offs.shape[0], dtype=I32)
    return rank + jnp.sum(jnp.where(idx[..., None] == experts, offs, 0), axis=-1)


def kernel(x_prompt, x_sample, cache_kv, cache_kr, state_ssm_re, state_ssm_im, page_table, w_in, q_norm_g, w_uq, kv_norm_g, w_uk, w_uv, ssm_log_dt, ssm_a_re, ssm_a_im, ssm_b_re, ssm_b_im, ssm_c_re, ssm_c_im, ssm_d, ssm_w_glu, ssm_b_glu, gmlp_ln_g, gmlp_ln_b, gmlp_w_s, gmlp_b_s, w_o, ln1_g, ln1_b, router_w, router_bias, exp_w_gate, exp_w_up, exp_w_down, sh_w_gate, sh_w_up, sh_w_down, ln2_g, ln2_b):
    depth = w_in.shape[0]
    batch, seq, d = x_prompt.shape
    dec_b, s_new, _ = x_sample.shape
    n_pages, page = page_table.shape[1], cache_kv.shape[2]
    past = n_pages * page
    tp, ts = batch * seq, dec_b * s_new
    t = tp + ts
    kv_rank, heads, nope = w_uk.shape[1:]
    rope = cache_kr.shape[3]
    n_g, n_p = ssm_a_re.shape[1:]
    n_e = router_w.shape[2]
    assert tp % ROW_TILE == 0 and ts % ROW_TILE == 0 and seq % ROW_TILE == 0 and ROW_TILE % s_new == 0
    assert seq % KV_TILE == 0 and KV_TILE % Q_TILE == 0 and t % MOE_TOKENS == 0 and heads % Q_CHUNK_HEADS == 0
    assert n_pages % PAGE_GROUP == 0 and KEY_CHUNK % page == 0 and (PAGE_GROUP * page) % KEY_CHUNK == 0
    assert dec_b * (n_pages // PAGE_GROUP) >= PAGE_SLOTS
    assert kv_rank + rope <= QK_PAD and s_new <= LANES_V7X
    alpha = (2.0 * depth) ** 0.25
    qscale = (nope + rope) ** -0.5 * math.log2(math.e)
    n_rows = t * TOP_K + n_e * MOE_ROWS
    cache_krt = jnp.swapaxes(cache_kr, 2, 3)

    small = (w_in, q_norm_g, w_uq, kv_norm_g, w_uk, w_uv, ssm_log_dt, ssm_a_re, ssm_a_im, ssm_b_re, ssm_b_im,
             ssm_c_re, ssm_c_im, ssm_d, ssm_w_glu, ssm_b_glu, gmlp_ln_g, gmlp_ln_b, gmlp_w_s, gmlp_b_s, w_o, ln1_g,
             ln1_b, router_w, router_bias, sh_w_gate, sh_w_up, sh_w_down, ln2_g, ln2_b)
    pos = jnp.concatenate([jnp.tile(jnp.arange(seq, dtype=I32), batch),
                           jnp.tile(past + jnp.arange(s_new, dtype=I32), dec_b)])
    tabs = _rope_tables(pos, rope)
    x_p, x_s = x_prompt.reshape(tp, d), x_sample.reshape(ts, d)
    h0_p = jnp.zeros((batch, 1, 2 * n_g * n_p), F32)
    row_sub = d // 2 // LANES_V7X
    xs = jnp.zeros((n_rows * row_sub, LANES_V7X), U32)
    per_layer = []
    for l in range(depth):
        lw = _layer_weights(l, small, s_new)
        cq, c, kr, kcat, u, gm, v = _in_proj(x_p, x_s, lw, tabs)
        q_p = _mla_q(cq, lw, tabs, 0, tp, BF16, qscale)
        q_s = _mla_q(cq, lw, tabs, tp, ts, F32, qscale)
        attn_p = _attn_prompt(q_p, kcat, lw["wuv"], batch, seq)
        attn_s = _attn_sample(q_s, c, kr, tp, lw["wuv"], cache_kv, cache_krt, page_table, l)
        h0_s = jnp.concatenate([state_ssm_re[l].reshape(dec_b, 1, -1), state_ssm_im[l].reshape(dec_b, 1, -1)], axis=2)
        ssm_p, hT_p = _s5(u, h0_p, lw, 0, tp, seq)
        ssm_s, hT_s = _s5(u, h0_s, lw, tp, ts, s_new)
        x1, xp = _out_proj(x_p, x_s, attn_p, attn_s, ssm_p, ssm_s, gm, lw, alpha)
        idx, gates, rank, cnt = _router(x1, lw)
        offs, block_e, n_blocks, next_e = _block_table(cnt[:, 0].astype(I32), n_rows)
        dest = _row_slots(idx, rank, offs)
        xs = _dispatch(xp, dest, xs, row_sub)
        yb = _experts(xs, block_e, n_blocks, next_e, exp_w_gate, exp_w_up, exp_w_down, l, row_sub)
        x_p, x_s = _combine(x1, gates, dest, yb, lw, alpha, row_sub, tp)
        per_layer.append((c, kr, hT_p, hT_s, v))

    half = n_g * n_p
    st = lambda f: jnp.stack([f(p) for p in per_layer], axis=0)
    return (
        x_p.reshape(batch, seq, d),
        x_s.reshape(dec_b, s_new, d),
        st(lambda p: p[0][:tp].reshape(batch, seq, kv_rank)),
        st(lambda p: p[1][:tp].reshape(batch, seq, rope)),
        st(lambda p: p[2][:, 0, :half].reshape(batch, n_g, n_p)),
        st(lambda p: p[2][:, 0, half:].reshape(batch, n_g, n_p)),
        st(lambda p: p[0][tp:].reshape(dec_b, s_new, kv_rank)),
        st(lambda p: p[1][tp:].reshape(dec_b, s_new, rope)),
        st(lambda p: p[3][:, 0, :half].reshape(dec_b, n_g, n_p)),
        st(lambda p: p[3][:, 0, half:].reshape(dec_b, n_g, n_p)),
        st(lambda p: p[4][tp:].reshape(dec_b, s_new, -1)),
    )
```
